```python
import jax, jax.numpy as jnp
from jax import lax
import numpy as np

D_MODEL = 2048
BATCH = 4
SEQ = 2048
DEPTH = 4

N_MIXERS = 2
N_RET_LAYERS = (DEPTH + 1) // 2
N_MLA_LAYERS = DEPTH // 2

RET_QK_DIM = 256
RET_HEADS = D_MODEL // RET_QK_DIM
RET_V_DIM = 2 * RET_QK_DIM
RET_QK = RET_HEADS * RET_QK_DIM
RET_VD = RET_HEADS * RET_V_DIM
RET_IN = 2 * RET_QK + 2 * RET_VD
RET_CHUNK = 128

MLA_NOPE = 128
MLA_ROPE = 64
MLA_V = 128
MLA_HEADS = D_MODEL // MLA_V
MLA_Q_LORA = D_MODEL // 4
MLA_KV_LORA = 512
MLA_IN = MLA_Q_LORA + MLA_KV_LORA + MLA_ROPE
MLA_Q_BLOCK = 128

D_FF = -(-8 * D_MODEL // (3 * 256)) * 256

ROPE_THETA = 10000.0
NORM_EPS = 1e-6
POS_OFFSET_MAX = 4096

kernel_name = "hybrid_retention_mla_swiglu_sandwich"


def rms_norm(x, g):
    xf = x.astype(jnp.float32)
    y = xf * lax.rsqrt(jnp.mean(xf * xf, axis=-1, keepdims=True) + NORM_EPS)
    return (y * g.astype(jnp.float32)).astype(x.dtype)


def rope(x, positions):
    d = x.shape[-1]
    inv = ROPE_THETA ** (-jnp.arange(0, d, 2, dtype=jnp.float32) / d)
    ang = positions.astype(jnp.float32)[..., None] * inv
    cos = jnp.cos(ang)[:, :, None, :]
    sin = jnp.sin(ang)[:, :, None, :]
    xf = x.astype(jnp.float32)
    x1, x2 = xf[..., : d // 2], xf[..., d // 2:]
    out = jnp.concatenate([x1 * cos - x2 * sin, x1 * sin + x2 * cos], axis=-1)
    return out.astype(x.dtype)


def retention(x, positions, w_in, gn_g, w_out):
    B, S, _ = x.shape
    H, dk, dv, C = RET_HEADS, RET_QK_DIM, RET_V_DIM, RET_CHUNK
    n_chunks = S // C
    proj = x @ w_in
    q, k, v, g = jnp.split(proj, [RET_QK, 2 * RET_QK, 2 * RET_QK + RET_VD], axis=-1)
    q = rope(q.reshape(B, S, H, dk), positions)
    k = rope(k.reshape(B, S, H, dk), positions) * (dk ** -0.5)
    v = v.reshape(B, S, H, dv)

    def to_chunks(t):
        return t.astype(jnp.float32).reshape(B, n_chunks, C, H, -1).transpose(1, 0, 3, 2, 4)

    qc, kc, vc = to_chunks(q), to_chunks(k), to_chunks(v)

    log_gamma = jnp.log1p(-jnp.exp2(-5.0 - jnp.arange(H, dtype=jnp.float32)))
    idx = jnp.arange(C, dtype=jnp.float32)
    rel = idx[:, None] - idx[None, :]
    causal = rel >= 0
    decay_in = jnp.where(causal, jnp.exp(log_gamma[:, None, None] * jnp.where(causal, rel, 0.0)), 0.0)
    decay_q = jnp.exp(log_gamma[:, None] * (idx + 1.0))[..., None]
    decay_k = jnp.exp(log_gamma[:, None] * (C - 1.0 - idx))[..., None]
    decay_chunk = jnp.exp(log_gamma * C)[:, None, None]

    def step(state, inp):
        qi, ki, vi = inp
        scores = jnp.einsum('bhqd,bhkd->bhqk', qi, ki) * decay_in
        y = (jnp.einsum('bhqk,bhkv->bhqv', scores, vi)
             + jnp.einsum('bhqd,bhdv->bhqv', qi, state) * decay_q)
        state = state * decay_chunk + jnp.einsum('bhkd,bhkv->bhdv', ki * decay_k, vi)
        return state, y

    state0 = jnp.zeros((B, H, dk, dv), jnp.float32)
    _, y = lax.scan(step, state0, (qc, kc, vc))
    y = y.transpose(1, 0, 3, 2, 4).reshape(B, S, H, dv)
    mu = jnp.mean(y, axis=-1, keepdims=True)
    var = jnp.mean(jnp.square(y - mu), axis=-1, keepdims=True)
    y = ((y - mu) * lax.rsqrt(var + NORM_EPS)).reshape(B, S, RET_VD) * gn_g.astype(jnp.float32)
    out = (jax.nn.silu(g.astype(jnp.float32)) * y).astype(x.dtype)
    return out @ w_out


def mla(x, positions, w_in, g_q, g_kv, w_uq, w_ukv, w_out):
    B, S, _ = x.shape
    H = MLA_HEADS
    dq = MLA_NOPE + MLA_ROPE
    c = x @ w_in
    c_q, c_kv, k_r = jnp.split(c, [MLA_Q_LORA, MLA_Q_LORA + MLA_KV_LORA], axis=-1)
    c_q = rms_norm(c_q, g_q)
    c_kv = rms_norm(c_kv, g_kv)
    q = (c_q @ w_uq).reshape(B, S, H, dq)
    q = jnp.concatenate([q[..., :MLA_NOPE], rope(q[..., MLA_NOPE:], positions)], axis=-1) * (dq ** -0.5)
    kv = (c_kv @ w_ukv).reshape(B, S, H, MLA_NOPE + MLA_V)
    k_nope, v = kv[..., :MLA_NOPE], kv[..., MLA_NOPE:]
    k_rope = rope(k_r[:, :, None, :], positions)
    k = jnp.concatenate([k_nope, jnp.broadcast_to(k_rope, (B, S, H, MLA_ROPE))], axis=-1)

    n_blocks = S // MLA_Q_BLOCK
    qb = q.reshape(B, n_blocks, MLA_Q_BLOCK, H, dq).transpose(1, 0, 2, 3, 4)
    kpos = jnp.arange(S)

    def attend_block(args):
        qi, bi = args
        s = jnp.einsum('bqhd,bkhd->bhqk', qi, k).astype(jnp.float32)
        qpos = bi * MLA_Q_BLOCK + jnp.arange(MLA_Q_BLOCK)
        s = jnp.where(kpos[None, :] <= qpos[:, None], s, -jnp.inf)
        p = jax.nn.softmax(s, axis=-1).astype(v.dtype)
        return jnp.einsum('bhqk,bkhv->bqhv', p, v)

    o = lax.map(attend_block, (qb, jnp.arange(n_blocks)))
    o = o.transpose(1, 0, 2, 3, 4).reshape(B, S, H * MLA_V)
    return o @ w_out


def swiglu(x, w_gu, w_down):
    gate, up = jnp.split(x @ w_gu, [D_FF], axis=-1)
    return (jax.nn.silu(gate) * up) @ w_down


def setup_inputs(seed: int = 0) -> dict:
    key = jax.random.key(seed)
    ks = jax.random.split(key, 17)
    f32 = jnp.float32

    def w(k, shape, fan_in):
        return jax.random.normal(k, shape, f32) * (fan_in ** -0.5)

    def gain(k, shape):
        return 1.0 + 0.02 * jax.random.normal(k, shape, f32)

    x = jax.random.normal(ks[0], (BATCH, SEQ, D_MODEL), f32)
    positions = (jnp.arange(SEQ, dtype=jnp.int32)[None, :]
                 + jax.random.randint(ks[1], (BATCH, 1), 0, POS_OFFSET_MAX, dtype=jnp.int32))
    return {
        "x": x,
        "positions": positions,
        "norm_mix_pre": gain(ks[2], (DEPTH, D_MODEL)),
        "norm_mix_post": gain(ks[3], (DEPTH, D_MODEL)),
        "norm_ffn_pre": gain(ks[4], (DEPTH, D_MODEL)),
        "norm_ffn_post": gain(ks[5], (DEPTH, D_MODEL)),
        "ret_w_in": w(ks[6], (N_RET_LAYERS, D_MODEL, RET_IN), D_MODEL),
        "ret_gn_g": gain(ks[7], (N_RET_LAYERS, RET_VD)),
        "ret_w_out": w(ks[8], (N_RET_LAYERS, RET_VD, D_MODEL), RET_VD),
        "mla_w_in": w(ks[9], (N_MLA_LAYERS, D_MODEL, MLA_IN), D_MODEL),
        "mla_g_q": gain(ks[10], (N_MLA_LAYERS, MLA_Q_LORA)),
        "mla_g_kv": gain(ks[11], (N_MLA_LAYERS, MLA_KV_LORA)),
        "mla_w_uq": w(ks[12], (N_MLA_LAYERS, MLA_Q_LORA, MLA_HEADS * (MLA_NOPE + MLA_ROPE)), MLA_Q_LORA),
        "mla_w_ukv": w(ks[13], (N_MLA_LAYERS, MLA_KV_LORA, MLA_HEADS * (MLA_NOPE + MLA_V)), MLA_KV_LORA),
        "mla_w_out": w(ks[14], (N_MLA_LAYERS, MLA_HEADS * MLA_V, D_MODEL), MLA_HEADS * MLA_V),
        "ffn_w_gu": w(ks[15], (DEPTH, D_MODEL, 2 * D_FF), D_MODEL),
        "ffn_w_down": w(ks[16], (DEPTH, D_FF, D_MODEL), D_FF),
    }


def reference(x, positions, norm_mix_pre, norm_mix_post, norm_ffn_pre, norm_ffn_post,
              ret_w_in, ret_gn_g, ret_w_out,
              mla_w_in, mla_g_q, mla_g_kv, mla_w_uq, mla_w_ukv, mla_w_out,
              ffn_w_gu, ffn_w_down):
    for i in range(DEPTH):
        j = i // N_MIXERS
        h = rms_norm(x, norm_mix_pre[i])
        if i % N_MIXERS == 0:
            h = retention(h, positions, ret_w_in[j], ret_gn_g[j], ret_w_out[j])
        else:
            h = mla(h, positions, mla_w_in[j], mla_g_q[j], mla_g_kv[j],
                    mla_w_uq[j], mla_w_ukv[j], mla_w_out[j])
        x = x + rms_norm(h, norm_mix_post[i])
        h = rms_norm(x, norm_ffn_pre[i])
        x = x + rms_norm(swiglu(h, ffn_w_gu[i], ffn_w_down[i]), norm_ffn_post[i])
    return x
```

```python
import functools

import jax
import jax.numpy as jnp
from jax import lax
from jax.experimental import pallas as pl
from jax.experimental.pallas import tpu as pltpu

F32 = jnp.float32
BF16 = jnp.bfloat16

RET_QK_DIM = 256
RET_V_DIM = 2 * RET_QK_DIM
MLA_NOPE = 128
MLA_ROPE = 64
MLA_V = 128
MLA_QD = MLA_NOPE + MLA_ROPE
ROPE_THETA = 10000.0
NORM_EPS = 1e-6
N_MIXERS = 2

V7X_VMEM_BYTES = 64 * 1024 * 1024
VMEM_LIMIT_BYTES = V7X_VMEM_BYTES - 8 * 1024 * 1024

ROW_TILE = 1024
NORM_CHUNK = 128
RET_CHUNK = 128
ATTN_TILE = 512
OUT_COL_CHUNK = 512
NEG_BIG = -1e30


def _params(*sem):
    return pltpu.CompilerParams(dimension_semantics=sem, vmem_limit_bytes=VMEM_LIMIT_BYTES)


def _rms(x, g):
    return x * lax.rsqrt(jnp.mean(x * x, axis=-1, keepdims=True) + NORM_EPS) * g


def _normalize_into(x_ref, g_ref, h_ref):
    g = g_ref[...]

    def body(c, carry):
        rows = pl.ds(pl.multiple_of(c * NORM_CHUNK, NORM_CHUNK), NORM_CHUNK)
        h_ref[rows, :] = _rms(x_ref[rows, :], g).astype(BF16)
        return carry

    lax.fori_loop(0, x_ref.shape[0] // NORM_CHUNK, body, 0)


def _bdot(a, b):
    return jnp.dot(a, b, preferred_element_type=F32)


def _silu(x):
    return x * jax.nn.sigmoid(x)


def _rope_table_kernel(pos_ref, inv_ref, cos_ref, sin_ref):
    ang = pos_ref[...].astype(F32) * inv_ref[...]
    cos_ref[...] = jnp.cos(ang)
    sin_ref[...] = jnp.sin(ang)


def _rope_tables(pos_col, d):
    t = pos_col.shape[0]
    f = d // 2
    inv = (ROPE_THETA ** (-jnp.arange(0, d, 2, dtype=F32) / d)).reshape(1, f)
    tm = ROW_TILE
    return pl.pallas_call(
        _rope_table_kernel,
        out_shape=(jax.ShapeDtypeStruct((t, f), F32), jax.ShapeDtypeStruct((t, f), F32)),
        grid=(t // tm,),
        in_specs=[pl.BlockSpec((tm, 1), lambda i: (i, 0)),
                  pl.BlockSpec((1, f), lambda i: (0, 0))],
        out_specs=(pl.BlockSpec((tm, f), lambda i: (i, 0)),
                   pl.BlockSpec((tm, f), lambda i: (i, 0))),
        compiler_params=_params("parallel"),
        name=f"rope_tables_{d}",
    )(pos_col, inv)


def _nm_plain_kernel(x_ref, g_ref, w_ref, o_ref, h_ref):
    @pl.when(pl.program_id(1) == 0)
    def _():
        _normalize_into(x_ref, g_ref, h_ref)

    o_ref[...] = _bdot(h_ref[...], w_ref[...].astype(BF16)).astype(o_ref.dtype)


def _nm_rope_kernel(x_ref, g_ref, w_ref, cos_ref, sin_ref, o_ref, h_ref, *, n_plain_tiles, late_scale):
    j = pl.program_id(1)

    @pl.when(j == 0)
    def _():
        _normalize_into(x_ref, g_ref, h_ref)

    acc = _bdot(h_ref[...], w_ref[...].astype(BF16))
    half = acc.shape[1] // 2
    x1, x2 = acc[:, :half], acc[:, half:]
    cos, sin = cos_ref[...], sin_ref[...]
    scale = jnp.where(j >= n_plain_tiles, F32(late_scale), F32(1.0))
    o_ref[:, :half] = ((x1 * cos - x2 * sin) * scale).astype(o_ref.dtype)
    o_ref[:, half:] = ((x1 * sin + x2 * cos) * scale).astype(o_ref.dtype)


def _nm_swiglu_kernel(x_ref, g_ref, wg_ref, wu_ref, o_ref, h_ref):
    @pl.when(pl.program_id(1) == 0)
    def _():
        _normalize_into(x_ref, g_ref, h_ref)

    h = h_ref[...]
    gate = _bdot(h, wg_ref[...].astype(BF16))
    up = _bdot(h, wu_ref[...].astype(BF16))
    o_ref[...] = (_silu(gate) * up).astype(o_ref.dtype)


def _norm_matmul(x, g, w, layer, *, col0, n_cols, tn, out_dtype, name):
    t, d = x.shape
    tm = ROW_TILE
    off = col0 // tn
    return pl.pallas_call(
        _nm_plain_kernel,
        out_shape=jax.ShapeDtypeStruct((t, n_cols), out_dtype),
        grid=(t // tm, n_cols // tn),
        in_specs=[pl.BlockSpec((tm, d), lambda i, j: (i, 0)),
                  pl.BlockSpec((1, d), lambda i, j: (0, 0)),
                  pl.BlockSpec((None, d, tn), lambda i, j: (layer, 0, j + off))],
        out_specs=pl.BlockSpec((tm, tn), lambda i, j: (i, j)),
        scratch_shapes=[pltpu.VMEM((tm, d), BF16)],
        compiler_params=_params("parallel", "arbitrary"),
        name=name,
    )(x, g, w)


def _norm_matmul_rope(x, g, w, layer, cos, sin, *, n_cols, head_dim, n_plain_tiles, late_scale, name):
    t, d = x.shape
    tm = ROW_TILE
    tn = head_dim
    kern = functools.partial(_nm_rope_kernel, n_plain_tiles=n_plain_tiles, late_scale=late_scale)
    return pl.pallas_call(
        kern,
        out_shape=jax.ShapeDtypeStruct((t, n_cols), F32),
        grid=(t // tm, n_cols // tn),
        in_specs=[pl.BlockSpec((tm, d), lambda i, j: (i, 0)),
                  pl.BlockSpec((1, d), lambda i, j: (0, 0)),
                  pl.BlockSpec((None, d, tn), lambda i, j: (layer, 0, j)),
                  pl.BlockSpec((tm, tn // 2), lambda i, j: (i, 0)),
                  pl.BlockSpec((tm, tn // 2), lambda i, j: (i, 0))],
        out_specs=pl.BlockSpec((tm, tn), lambda i, j: (i, j)),
        scratch_shapes=[pltpu.VMEM((tm, d), BF16)],
        compiler_params=_params("parallel", "arbitrary"),
        name=name,
    )(x, g, w, cos, sin)


def _norm_swiglu(x, g, w_gu, layer, *, tn, name):
    t, d = x.shape
    d_ff = w_gu.shape[2] // 2
    tm = ROW_TILE
    n_tiles = d_ff // tn
    return pl.pallas_call(
        _nm_swiglu_kernel,
        out_shape=jax.ShapeDtypeStruct((t, d_ff), BF16),
        grid=(t // tm, n_tiles),
        in_specs=[pl.BlockSpec((tm, d), lambda i, j: (i, 0)),
                  pl.BlockSpec((1, d), lambda i, j: (0, 0)),
                  pl.BlockSpec((None, d, tn), lambda i, j: (layer, 0, j)),
                  pl.BlockSpec((None, d, tn), lambda i, j: (layer, 0, j + n_tiles))],
        out_specs=pl.BlockSpec((tm, tn), lambda i, j: (i, j)),
        scratch_shapes=[pltpu.VMEM((tm, d), BF16)],
        compiler_params=_params("parallel", "arbitrary"),
        name=name,
    )(x, g, w_gu, w_gu)


def _mm_norm_res_kernel(a_ref, w_ref, x_ref, g_ref, o_ref, *, n_k):
    k = pl.program_id(1)

    @pl.when(k == 0)
    def _():
        o_ref[...] = jnp.zeros_like(o_ref)

    a = a_ref[...]
    for n in range(0, o_ref.shape[1], OUT_COL_CHUNK):
        cols = slice(n, n + OUT_COL_CHUNK)
        o_ref[:, cols] += _bdot(a, w_ref[:, cols].astype(BF16))

    @pl.when(k == n_k - 1)
    def _():
        o_ref[...] = x_ref[...] + _rms(o_ref[...], g_ref[...])


def _matmul_norm_residual(a, w, layer, x, g, *, tk, name):
    t, kdim = a.shape
    d = w.shape[2]
    tm = ROW_TILE
    n_k = kdim // tk
    kern = functools.partial(_mm_norm_res_kernel, n_k=n_k)
    return pl.pallas_call(
        kern,
        out_shape=jax.ShapeDtypeStruct((t, d), F32),
        grid=(t // tm, n_k),
        in_specs=[pl.BlockSpec((tm, tk), lambda i, k: (i, k)),
                  pl.BlockSpec((None, tk, d), lambda i, k: (layer, k, 0)),
                  pl.BlockSpec((tm, d), lambda i, k: (i, 0)),
                  pl.BlockSpec((1, d), lambda i, k: (0, 0))],
        out_specs=pl.BlockSpec((tm, d), lambda i, k: (i, 0)),
        compiler_params=_params("parallel", "arbitrary"),
        name=name,
    )(a, w, x, g)


def _retention_kernel(lg_ref, q_ref, k_ref, v_ref, g_ref, gn_ref, o_ref, state_ref):
    c_len = RET_CHUNK
    n_chunks = q_ref.shape[0] // c_len
    lg = jnp.full((1, 1), lg_ref[pl.program_id(1)], F32)
    state_ref[...] = jnp.zeros_like(state_ref)

    row = lax.broadcasted_iota(jnp.int32, (c_len, c_len), 0)
    col = lax.broadcasted_iota(jnp.int32, (c_len, c_len), 1)
    rel = (row - col).astype(F32)
    causal = rel >= 0
    decay_in = jnp.where(causal, jnp.exp(lg * jnp.where(causal, rel, 0.0)), 0.0)
    idx = lax.broadcasted_iota(jnp.int32, (c_len, 1), 0).astype(F32)
    decay_q = jnp.exp(lg * (idx + 1.0))
    decay_k = jnp.exp(lg * (c_len - 1.0 - idx))
    decay_chunk = jnp.exp(lg * c_len)
    gn = gn_ref[...]

    def body(c, carry):
        rows = pl.ds(pl.multiple_of(c * c_len, c_len), c_len)
        q = q_ref[rows, :]
        k = k_ref[rows, :]
        v = v_ref[rows, :]
        qb = q.astype(BF16)
        scores = lax.dot_general(qb, k.astype(BF16), (((1,), (1,)), ((), ())),
                                 preferred_element_type=F32) * decay_in
        state = state_ref[...]
        y = _bdot(scores.astype(BF16), v) + _bdot(qb, state.astype(BF16)) * decay_q
        kd_t = (k * decay_k).T.astype(BF16)
        state_ref[...] = state * decay_chunk + _bdot(kd_t, v)

        mu = jnp.mean(y, axis=-1, keepdims=True)
        yc = y - mu
        var = jnp.mean(yc * yc, axis=-1, keepdims=True)
        yn = yc * lax.rsqrt(var + NORM_EPS) * gn
        o_ref[rows, :] = (_silu(g_ref[rows, :]) * yn).astype(o_ref.dtype)
        return carry

    lax.fori_loop(0, n_chunks, body, 0)


def _retention_core(qk, v, gate, gn_g, log_gamma, *, batch, seq):
    t = qk.shape[0]
    dk, dv = RET_QK_DIM, RET_V_DIM
    heads = v.shape[1] // dv
    return pl.pallas_call(
        _retention_kernel,
        out_shape=jax.ShapeDtypeStruct((t, heads * dv), BF16),
        grid=(batch, heads),
        in_specs=[pl.BlockSpec(memory_space=pltpu.SMEM),
                  pl.BlockSpec((seq, dk), lambda b, h: (b, h)),
                  pl.BlockSpec((seq, dk), lambda b, h: (b, heads + h)),
                  pl.BlockSpec((seq, dv), lambda b, h: (b, h)),
                  pl.BlockSpec((seq, dv), lambda b, h: (b, h)),
                  pl.BlockSpec((1, dv), lambda b, h: (0, h))],
        out_specs=pl.BlockSpec((seq, dv), lambda b, h: (b, h)),
        scratch_shapes=[pltpu.VMEM((dk, dv), F32)],
        compiler_params=_params("parallel", "parallel"),
        name="retention_core",
    )(log_gamma, qk, qk, v, gate, gn_g)


def _mla_in_kernel(x_ref, g_ref, w_ref, gq_ref, gkv_ref, cos_ref, sin_ref,
                   cq_ref, ckv_ref, kr_ref, *, q_lora, kv_lora):
    h = _rms(x_ref[...], g_ref[...]).astype(BF16)
    c = _bdot(h, w_ref[...].astype(BF16))
    cq_ref[...] = _rms(c[:, :q_lora], gq_ref[...]).astype(cq_ref.dtype)
    ckv_ref[...] = _rms(c[:, q_lora:q_lora + kv_lora], gkv_ref[...]).astype(ckv_ref.dtype)
    kr = c[:, q_lora + kv_lora:]
    half = kr.shape[1] // 2
    x1, x2 = kr[:, :half], kr[:, half:]
    cos, sin = cos_ref[...], sin_ref[...]
    kr_ref[...] = jnp.concatenate([x1 * cos - x2 * sin, x1 * sin + x2 * cos], axis=-1)


def _mla_in(x, g, w, layer, gq, gkv, cos, sin):
    t, d = x.shape
    q_lora, kv_lora = gq.shape[1], gkv.shape[1]
    n_in = w.shape[2]
    rope = n_in - q_lora - kv_lora
    tm = ROW_TILE // 2
    kern = functools.partial(_mla_in_kernel, q_lora=q_lora, kv_lora=kv_lora)
    return pl.pallas_call(
        kern,
        out_shape=(jax.ShapeDtypeStruct((t, q_lora), BF16),
                   jax.ShapeDtypeStruct((t, kv_lora), BF16),
                   jax.ShapeDtypeStruct((t, rope), F32)),
        grid=(t // tm,),
        in_specs=[pl.BlockSpec((tm, d), lambda i: (i, 0)),
                  pl.BlockSpec((1, d), lambda i: (0, 0)),
                  pl.BlockSpec((None, d, n_in), lambda i: (layer, 0, 0)),
                  pl.BlockSpec((1, q_lora), lambda i: (0, 0)),
                  pl.BlockSpec((1, kv_lora), lambda i: (0, 0)),
                  pl.BlockSpec((tm, rope // 2), lambda i: (i, 0)),
                  pl.BlockSpec((tm, rope // 2), lambda i: (i, 0))],
        out_specs=(pl.BlockSpec((tm, q_lora), lambda i: (i, 0)),
                   pl.BlockSpec((tm, kv_lora), lambda i: (i, 0)),
                   pl.BlockSpec((tm, rope), lambda i: (i, 0))),
        compiler_params=_params("parallel"),
        name="mla_in",
    )(x, g, w, gq, gkv, cos, sin)


def _mla_q_kernel(cq_ref, w_ref, cos_ref, sin_ref, o_ref, *, scale):
    q2 = _bdot(cq_ref[...], w_ref[...].astype(BF16))
    cos, sin = cos_ref[...], sin_ref[...]
    half = MLA_ROPE // 2
    for t in range(o_ref.shape[0]):
        qh = q2[:, t * MLA_QD:(t + 1) * MLA_QD]
        x1 = qh[:, MLA_NOPE:MLA_NOPE + half]
        x2 = qh[:, MLA_NOPE + half:]
        full = jnp.concatenate([qh[:, :MLA_NOPE], x1 * cos - x2 * sin, x1 * sin + x2 * cos], axis=-1)
        o_ref[t] = (full * scale).astype(o_ref.dtype)


def _mla_q(cq, w_uq, layer, cos, sin):
    t, q_lora = cq.shape
    heads = w_uq.shape[2] // MLA_QD
    tm = ROW_TILE
    hp = 2
    kern = functools.partial(_mla_q_kernel, scale=MLA_QD ** -0.5)
    return pl.pallas_call(
        kern,
        out_shape=jax.ShapeDtypeStruct((heads, t, MLA_QD), BF16),
        grid=(t // tm, heads // hp),
        in_specs=[pl.BlockSpec((tm, q_lora), lambda i, h: (i, 0)),
                  pl.BlockSpec((None, q_lora, hp * MLA_QD), lambda i, h: (layer, 0, h)),
                  pl.BlockSpec((tm, MLA_ROPE // 2), lambda i, h: (i, 0)),
                  pl.BlockSpec((tm, MLA_ROPE // 2), lambda i, h: (i, 0))],
        out_specs=pl.BlockSpec((hp, tm, MLA_QD), lambda i, h: (h, i, 0)),
        compiler_params=_params("parallel", "parallel"),
        name="mla_q",
    )(cq, w_uq, cos, sin)


def _mla_kv_kernel(ckv_ref, w_ref, kr_ref, k_ref, v_ref):
    kv = _bdot(ckv_ref[...], w_ref[...].astype(BF16))
    k_ref[...] = jnp.concatenate([kv[:, :MLA_NOPE], kr_ref[...]], axis=-1).astype(k_ref.dtype)
    v_ref[...] = kv[:, MLA_NOPE:].astype(v_ref.dtype)


def _mla_kv(ckv, w_ukv, layer, k_rope):
    t, kv_lora = ckv.shape
    per_head = MLA_NOPE + MLA_V
    heads = w_ukv.shape[2] // per_head
    tm = ROW_TILE
    return pl.pallas_call(
        _mla_kv_kernel,
        out_shape=(jax.ShapeDtypeStruct((heads, t, MLA_QD), BF16),
                   jax.ShapeDtypeStruct((heads, t, MLA_V), BF16)),
        grid=(t // tm, heads),
        in_specs=[pl.BlockSpec((tm, kv_lora), lambda i, h: (i, 0)),
                  pl.BlockSpec((None, kv_lora, per_head), lambda i, h: (layer, 0, h)),
                  pl.BlockSpec((tm, MLA_ROPE), lambda i, h: (i, 0))],
        out_specs=(pl.BlockSpec((None, tm, MLA_QD), lambda i, h: (h, i, 0)),
                   pl.BlockSpec((None, tm, MLA_V), lambda i, h: (h, i, 0))),
        compiler_params=_params("parallel", "parallel"),
        name="mla_kv",
    )(ckv, w_ukv, k_rope)


def _flash_kernel(qi_ref, ki_ref, q_ref, k_ref, v_ref, o_ref, m_ref, l_ref, acc_ref):
    step = pl.program_id(2)
    qi = qi_ref[step]
    ki = ki_ref[step]

    @pl.when(ki == 0)
    def _():
        m_ref[...] = jnp.full_like(m_ref, NEG_BIG)
        l_ref[...] = jnp.zeros_like(l_ref)
        acc_ref[...] = jnp.zeros_like(acc_ref)

    def update(masked):
        s = lax.dot_general(q_ref[...], k_ref[...], (((1,), (1,)), ((), ())),
                            preferred_element_type=F32)
        if masked:
            row = lax.broadcasted_iota(jnp.int32, s.shape, 0)
            col = lax.broadcasted_iota(jnp.int32, s.shape, 1)
            s = jnp.where(col <= row, s, NEG_BIG)
        m_prev = m_ref[...]
        m_new = jnp.maximum(m_prev, jnp.max(s, axis=-1, keepdims=True))
        alpha = jnp.exp(m_prev - m_new)
        p = jnp.exp(s - m_new)
        l_ref[...] = alpha * l_ref[...] + jnp.sum(p, axis=-1, keepdims=True)
        acc_ref[...] = alpha * acc_ref[...] + _bdot(p.astype(BF16), v_ref[...])
        m_ref[...] = m_new

    @pl.when(ki < qi)
    def _():
        update(False)

    @pl.when(ki == qi)
    def _():
        update(True)
        o_ref[...] = (acc_ref[...] / l_ref[...]).astype(o_ref.dtype)


def _flash_attention(q, k, v, *, batch, seq):
    heads, t, _ = q.shape
    tile = ATTN_TILE
    nq = seq // tile
    pairs = [(a, b) for a in range(nq) for b in range(a + 1)]
    qi_tab = jnp.asarray([p[0] for p in pairs], jnp.int32)
    ki_tab = jnp.asarray([p[1] for p in pairs], jnp.int32)
    return pl.pallas_call(
        _flash_kernel,
        out_shape=jax.ShapeDtypeStruct((t, heads * MLA_V), BF16),
        grid_spec=pltpu.PrefetchScalarGridSpec(
            num_scalar_prefetch=2,
            grid=(batch, heads, len(pairs)),
            in_specs=[pl.BlockSpec((None, tile, MLA_QD), lambda b, h, s, qt, kt: (h, b * nq + qt[s], 0)),
                      pl.BlockSpec((None, tile, MLA_QD), lambda b, h, s, qt, kt: (h, b * nq + kt[s], 0)),
                      pl.BlockSpec((None, tile, MLA_V), lambda b, h, s, qt, kt: (h, b * nq + kt[s], 0))],
            out_specs=pl.BlockSpec((tile, MLA_V), lambda b, h, s, qt, kt: (b * nq + qt[s], h)),
            scratch_shapes=[pltpu.VMEM((tile, 1), F32),
                            pltpu.VMEM((tile, 1), F32),
                            pltpu.VMEM((tile, MLA_V), F32)]),
        compiler_params=_params("parallel", "parallel", "arbitrary"),
        name="mla_flash_attention",
    )(qi_tab, ki_tab, q, k, v)


def _retention_layer(x, g_pre, g_post, w_in, gn_g, w_out, layer, cos, sin, log_gamma, *, batch, seq):
    d = x.shape[1]
    dk, dv = RET_QK_DIM, RET_V_DIM
    heads = d // dk
    n_qk = 2 * heads * dk
    n_v = heads * dv
    qk = _norm_matmul_rope(x, g_pre, w_in, layer, cos, sin, n_cols=n_qk, head_dim=dk,
                           n_plain_tiles=heads, late_scale=dk ** -0.5, name="ret_qk_proj")
    v = _norm_matmul(x, g_pre, w_in, layer, col0=n_qk, n_cols=n_v, tn=512, out_dtype=BF16,
                     name="ret_v_proj")
    gate = _norm_matmul(x, g_pre, w_in, layer, col0=n_qk + n_v, n_cols=n_v, tn=512, out_dtype=F32,
                        name="ret_gate_proj")
    y = _retention_core(qk, v, gate, gn_g, log_gamma, batch=batch, seq=seq)
    return _matmul_norm_residual(y, w_out, layer, x, g_post, tk=512, name="ret_out_proj")


def _mla_layer(x, g_pre, g_post, w_in, g_q, g_kv, w_uq, w_ukv, w_out, layer, cos, sin, *, batch, seq):
    cq, ckv, k_rope = _mla_in(x, g_pre, w_in, layer, g_q, g_kv, cos, sin)
    q = _mla_q(cq, w_uq, layer, cos, sin)
    k, v = _mla_kv(ckv, w_ukv, layer, k_rope)
    o = _flash_attention(q, k, v, batch=batch, seq=seq)
    return _matmul_norm_residual(o, w_out, layer, x, g_post, tk=512, name="mla_out_proj")


def _ffn(x, g_pre, g_post, w_gu, w_down, layer):
    act = _norm_swiglu(x, g_pre, w_gu, layer, tn=512, name="ffn_gate_up")
    return _matmul_norm_residual(act, w_down, layer, x, g_post, tk=512, name="ffn_down_proj")


def kernel(x, positions, norm_mix_pre, norm_mix_post, norm_ffn_pre, norm_ffn_post, ret_w_in, ret_gn_g, ret_w_out, mla_w_in, mla_g_q, mla_g_kv, mla_w_uq, mla_w_ukv, mla_w_out, ffn_w_gu, ffn_w_down):
    batch, seq, d = x.shape
    depth = norm_mix_pre.shape[0]
    t = batch * seq
    xf = x.reshape(t, d)
    pos_col = positions.reshape(t, 1)
    cos_r, sin_r = _rope_tables(pos_col, RET_QK_DIM)
    cos_m, sin_m = _rope_tables(pos_col, MLA_ROPE)
    ret_heads = d // RET_QK_DIM
    log_gamma = jnp.log1p(-jnp.exp2(-5.0 - jnp.arange(ret_heads, dtype=F32)))

    def row(a, i):
        return a[i].reshape(1, -1)

    for i in range(depth):
        j = i // N_MIXERS
        if i % N_MIXERS == 0:
            xf = _retention_layer(xf, row(norm_mix_pre, i), row(norm_mix_post, i), ret_w_in,
                                  row(ret_gn_g, j), ret_w_out, j, cos_r, sin_r, log_gamma,
                                  batch=batch, seq=seq)
        else:
            xf = _mla_layer(xf, row(norm_mix_pre, i), row(norm_mix_post, i), mla_w_in,
                            row(mla_g_q, j), row(mla_g_kv, j), mla_w_uq, mla_w_ukv,
                            mla_w_out, j, cos_m, sin_m, batch=batch, seq=seq)
        xf = _ffn(xf, row(norm_ffn_pre, i), row(norm_ffn_post, i), ffn_w_gu, ffn_w_down, i)
    return xf.reshape(batch, seq, d)
```

```python
import functools

import jax
import jax.numpy as jnp
from jax import lax
from jax.experimental import pallas as pl
from jax.experimental.pallas import tpu as pltpu

F32 = jnp.float32
BF16 = jnp.bfloat16

RET_QK_DIM = 256
RET_V_DIM = 2 * RET_QK_DIM
MLA_NOPE = 128
MLA_ROPE = 64
MLA_V = 128
MLA_QD = MLA_NOPE + MLA_ROPE
MLA_PAD = 256
LANES = 128
ROPE_THETA = 10000.0
NORM_EPS = 1e-6
N_MIXERS = 2

V7X_VMEM_BYTES = 64 * 1024 * 1024
VMEM_LIMIT_BYTES = V7X_VMEM_BYTES - 8 * 1024 * 1024

ROW_TILE = 1024
NORM_CHUNK = 128
RET_CHUNK = 256
RET_SEQ_BLOCK = 1024
RET_HEADS_PER_STEP = 2
MLA_HEADS_PER_STEP = 4
ATTN_TILE = 512
ATTN_HEADS_PER_STEP = 2
OUT_COL_CHUNK = 512
NEG_BIG = -1e30


def _params(*sem):
    return pltpu.CompilerParams(dimension_semantics=sem, vmem_limit_bytes=VMEM_LIMIT_BYTES)


def _rms(x, g):
    return x * lax.rsqrt(jnp.mean(x * x, axis=-1, keepdims=True) + NORM_EPS) * g


def _normalize_into(x_ref, g_ref, h_ref):
    g = g_ref[...]

    def body(c, carry):
        rows = pl.ds(pl.multiple_of(c * NORM_CHUNK, NORM_CHUNK), NORM_CHUNK)
        h_ref[rows, :] = _rms(x_ref[rows, :], g).astype(BF16)
        return carry

    lax.fori_loop(0, x_ref.shape[0] // NORM_CHUNK, body, 0)


def _bdot(a, b):
    return jnp.dot(a, b, preferred_element_type=F32)


def _silu(x):
    return x * jax.nn.sigmoid(x)


def _rope_table_kernel(pos_ref, inv_ref, cos_ref, sin_ref):
    ang = pos_ref[...].astype(F32) * inv_ref[...]
    cos_ref[...] = jnp.cos(ang)
    sin_ref[...] = jnp.sin(ang)


def _rope_tables(pos_col, d):
    t = pos_col.shape[0]
    f = d // 2
    inv = (ROPE_THETA ** (-jnp.arange(0, d, 2, dtype=F32) / d)).reshape(1, f)
    tm = ROW_TILE
    return pl.pallas_call(
        _rope_table_kernel,
        out_shape=(jax.ShapeDtypeStruct((t, f), F32), jax.ShapeDtypeStruct((t, f), F32)),
        grid=(t // tm,),
        in_specs=[pl.BlockSpec((tm, 1), lambda i: (i, 0)),
                  pl.BlockSpec((1, f), lambda i: (0, 0))],
        out_specs=(pl.BlockSpec((tm, f), lambda i: (i, 0)),
                   pl.BlockSpec((tm, f), lambda i: (i, 0))),
        compiler_params=_params("parallel"),
        name=f"rope_tables_{d}",
    )(pos_col, inv)


def _nm_plain_kernel(x_ref, g_ref, w_ref, o_ref, h_ref, *, activation):
    @pl.when(pl.program_id(1) == 0)
    def _():
        _normalize_into(x_ref, g_ref, h_ref)

    acc = _bdot(h_ref[...], w_ref[...].astype(BF16))
    if activation is not None:
        acc = activation(acc)
    o_ref[...] = acc.astype(o_ref.dtype)


def _nm_rope_kernel(x_ref, g_ref, w_ref, cos_ref, sin_ref, o_ref, h_ref, *, scale):
    @pl.when(pl.program_id(1) == 0)
    def _():
        _normalize_into(x_ref, g_ref, h_ref)

    acc = _bdot(h_ref[...], w_ref[...].astype(BF16))
    half = acc.shape[1] // 2
    x1, x2 = acc[:, :half], acc[:, half:]
    cos, sin = cos_ref[...], sin_ref[...]
    r1 = x1 * cos - x2 * sin
    r2 = x1 * sin + x2 * cos
    if scale is not None:
        r1, r2 = r1 * scale, r2 * scale
    o_ref[:, :half] = r1.astype(o_ref.dtype)
    o_ref[:, half:] = r2.astype(o_ref.dtype)


def _nm_swiglu_kernel(x_ref, g_ref, wg_ref, wu_ref, o_ref, h_ref):
    @pl.when(pl.program_id(1) == 0)
    def _():
        _normalize_into(x_ref, g_ref, h_ref)

    h = h_ref[...]
    gate = _bdot(h, wg_ref[...].astype(BF16))
    up = _bdot(h, wu_ref[...].astype(BF16))
    o_ref[...] = (_silu(gate) * up).astype(o_ref.dtype)


def _norm_matmul(x, g, w, layer, *, col0, n_cols, tn, out_dtype, name, activation=None):
    t, d = x.shape
    tm = ROW_TILE
    off = col0 // tn
    return pl.pallas_call(
        functools.partial(_nm_plain_kernel, activation=activation),
        out_shape=jax.ShapeDtypeStruct((t, n_cols), out_dtype),
        grid=(t // tm, n_cols // tn),
        in_specs=[pl.BlockSpec((tm, d), lambda i, j: (i, 0)),
                  pl.BlockSpec((1, d), lambda i, j: (0, 0)),
                  pl.BlockSpec((None, d, tn), lambda i, j: (layer, 0, j + off))],
        out_specs=pl.BlockSpec((tm, tn), lambda i, j: (i, j)),
        scratch_shapes=[pltpu.VMEM((tm, d), BF16)],
        compiler_params=_params("parallel", "arbitrary"),
        name=name,
    )(x, g, w)


def _norm_matmul_rope(x, g, w, layer, cos, sin, *, col0, n_cols, head_dim, scale, out_dtype, name):
    t, d = x.shape
    tm = ROW_TILE
    tn = head_dim
    off = col0 // tn
    kern = functools.partial(_nm_rope_kernel, scale=scale)
    return pl.pallas_call(
        kern,
        out_shape=jax.ShapeDtypeStruct((t, n_cols), out_dtype),
        grid=(t // tm, n_cols // tn),
        in_specs=[pl.BlockSpec((tm, d), lambda i, j: (i, 0)),
                  pl.BlockSpec((1, d), lambda i, j: (0, 0)),
                  pl.BlockSpec((None, d, tn), lambda i, j: (layer, 0, j + off)),
                  pl.BlockSpec((tm, tn // 2), lambda i, j: (i, 0)),
                  pl.BlockSpec((tm, tn // 2), lambda i, j: (i, 0))],
        out_specs=pl.BlockSpec((tm, tn), lambda i, j: (i, j)),
        scratch_shapes=[pltpu.VMEM((tm, d), BF16)],
        compiler_params=_params("parallel", "arbitrary"),
        name=name,
    )(x, g, w, cos, sin)


def _norm_swiglu(x, g, w_gu, layer, *, tn, name):
    t, d = x.shape
    d_ff = w_gu.shape[2] // 2
    tm = ROW_TILE
    n_tiles = d_ff // tn
    return pl.pallas_call(
        _nm_swiglu_kernel,
        out_shape=jax.ShapeDtypeStruct((t, d_ff), BF16),
        grid=(t // tm, n_tiles),
        in_specs=[pl.BlockSpec((tm, d), lambda i, j: (i, 0)),
                  pl.BlockSpec((1, d), lambda i, j: (0, 0)),
                  pl.BlockSpec((None, d, tn), lambda i, j: (layer, 0, j)),
                  pl.BlockSpec((None, d, tn), lambda i, j: (layer, 0, j + n_tiles))],
        out_specs=pl.BlockSpec((tm, tn), lambda i, j: (i, j)),
        scratch_shapes=[pltpu.VMEM((tm, d), BF16)],
        compiler_params=_params("parallel", "arbitrary"),
        name=name,
    )(x, g, w_gu, w_gu)


def _mm_norm_res_kernel(a_ref, w_ref, x_ref, g_ref, o_ref, *, n_k):
    k = pl.program_id(1)

    @pl.when(k == 0)
    def _():
        o_ref[...] = jnp.zeros_like(o_ref)

    a = a_ref[...]
    for n in range(0, o_ref.shape[1], OUT_COL_CHUNK):
        cols = slice(n, n + OUT_COL_CHUNK)
        o_ref[:, cols] += _bdot(a, w_ref[:, cols].astype(BF16))

    @pl.when(k == n_k - 1)
    def _():
        o_ref[...] = x_ref[...] + _rms(o_ref[...], g_ref[...])


def _matmul_norm_residual(a, w, layer, x, g, *, tk, name):
    t, kdim = a.shape
    d = w.shape[2]
    tm = ROW_TILE
    n_k = kdim // tk
    kern = functools.partial(_mm_norm_res_kernel, n_k=n_k)
    return pl.pallas_call(
        kern,
        out_shape=jax.ShapeDtypeStruct((t, d), F32),
        grid=(t // tm, n_k),
        in_specs=[pl.BlockSpec((tm, tk), lambda i, k: (i, k)),
                  pl.BlockSpec((None, tk, d), lambda i, k: (layer, k, 0)),
                  pl.BlockSpec((tm, d), lambda i, k: (i, 0)),
                  pl.BlockSpec((1, d), lambda i, k: (0, 0))],
        out_specs=pl.BlockSpec((tm, d), lambda i, k: (i, 0)),
        compiler_params=_params("parallel", "arbitrary"),
        name=name,
    )(a, w, x, g)


def _retention_kernel(lg_ref, q_ref, k_ref, v_ref, g_ref, gn_ref, o_ref, state_ref, *, heads_per_step):
    c_len = RET_CHUNK
    dk, dv = RET_QK_DIM, RET_V_DIM
    n_chunks = q_ref.shape[0] // c_len

    @pl.when(pl.program_id(2) == 0)
    def _():
        state_ref[...] = jnp.zeros_like(state_ref)

    row = lax.broadcasted_iota(jnp.int32, (c_len, c_len), 0)
    col = lax.broadcasted_iota(jnp.int32, (c_len, c_len), 1)
    rel = (row - col).astype(F32)
    causal = rel >= 0
    rel_pos = jnp.where(causal, rel, 0.0)
    idx = lax.broadcasted_iota(jnp.int32, (c_len, 1), 0).astype(F32)
    decays = []
    for g in range(heads_per_step):
        lg = jnp.full((1, 1), lg_ref[pl.program_id(1) * heads_per_step + g], F32)
        decays.append((jnp.where(causal, jnp.exp(lg * rel_pos), 0.0),
                       jnp.exp(lg * (idx + 1.0)),
                       jnp.exp(lg * (c_len - 1.0 - idx)),
                       jnp.exp(lg * c_len)))

    for c in range(n_chunks):
        rows = slice(c * c_len, (c + 1) * c_len)
        for g in range(heads_per_step):
            decay_in, decay_q, decay_k, decay_chunk = decays[g]
            qk_cols = slice(g * dk, (g + 1) * dk)
            v_cols = slice(g * dv, (g + 1) * dv)
            q = q_ref[rows, qk_cols]
            k = k_ref[rows, qk_cols]
            v = v_ref[rows, v_cols]
            scores = lax.dot_general(q, k.astype(BF16), (((1,), (1,)), ((), ())),
                                     preferred_element_type=F32) * decay_in
            state = state_ref[g]
            y = _bdot(scores.astype(BF16), v) + _bdot(q, state.astype(BF16)) * decay_q
            kd_t = (k * decay_k).T.astype(BF16)
            state_ref[g] = state * decay_chunk + _bdot(kd_t, v)

            mu = jnp.mean(y, axis=-1, keepdims=True)
            yc = y - mu
            var = jnp.mean(yc * yc, axis=-1, keepdims=True)
            yn = yc * lax.rsqrt(var + NORM_EPS) * gn_ref[:, v_cols]
            o_ref[rows, v_cols] = (g_ref[rows, v_cols] * yn).astype(o_ref.dtype)


def _retention_core(q, k, v, gate, gn_g, log_gamma, *, batch, seq):
    t = q.shape[0]
    dk, dv = RET_QK_DIM, RET_V_DIM
    heads = v.shape[1] // dv
    hp = RET_HEADS_PER_STEP
    lb = RET_SEQ_BLOCK
    nb = seq // lb
    kern = functools.partial(_retention_kernel, heads_per_step=hp)
    return pl.pallas_call(
        kern,
        out_shape=jax.ShapeDtypeStruct((t, heads * dv), BF16),
        grid=(batch, heads // hp, nb),
        in_specs=[pl.BlockSpec(memory_space=pltpu.SMEM),
                  pl.BlockSpec((lb, hp * dk), lambda b, h, s: (b * nb + s, h)),
                  pl.BlockSpec((lb, hp * dk), lambda b, h, s: (b * nb + s, h)),
                  pl.BlockSpec((lb, hp * dv), lambda b, h, s: (b * nb + s, h)),
                  pl.BlockSpec((lb, hp * dv), lambda b, h, s: (b * nb + s, h)),
                  pl.BlockSpec((1, hp * dv), lambda b, h, s: (0, h))],
        out_specs=pl.BlockSpec((lb, hp * dv), lambda b, h, s: (b * nb + s, h)),
        scratch_shapes=[pltpu.VMEM((hp, dk, dv), F32)],
        compiler_params=_params("parallel", "parallel", "arbitrary"),
        name="retention_core",
    )(log_gamma, q, k, v, gate, gn_g)


def _mla_rope_table_kernel(pos_ref, inv_ref, cos_mask_ref, sin_sign_ref, tcos_ref, tsin_ref):
    ang = pos_ref[...].astype(F32) * inv_ref[...]
    tcos_ref[...] = jnp.cos(ang) * cos_mask_ref[...]
    tsin_ref[...] = jnp.sin(ang) * sin_sign_ref[...]


def _mla_rope_tables(pos_col):
    t = pos_col.shape[0]
    half = MLA_ROPE // 2
    inv = ROPE_THETA ** (-jnp.arange(0, MLA_ROPE, 2, dtype=F32) / MLA_ROPE)
    ones, zeros = jnp.ones((half,), F32), jnp.zeros((half,), F32)
    inv_row = jnp.concatenate([inv, inv, zeros, zeros]).reshape(1, LANES)
    cos_mask = jnp.concatenate([ones, ones, zeros, zeros]).reshape(1, LANES)
    sin_sign = jnp.concatenate([-ones, ones, zeros, zeros]).reshape(1, LANES)
    tm = ROW_TILE
    row = pl.BlockSpec((1, LANES), lambda i: (0, 0))
    return pl.pallas_call(
        _mla_rope_table_kernel,
        out_shape=(jax.ShapeDtypeStruct((t, LANES), F32), jax.ShapeDtypeStruct((t, LANES), F32)),
        grid=(t // tm,),
        in_specs=[pl.BlockSpec((tm, 1), lambda i: (i, 0)), row, row, row],
        out_specs=(pl.BlockSpec((tm, LANES), lambda i: (i, 0)),
                   pl.BlockSpec((tm, LANES), lambda i: (i, 0))),
        compiler_params=_params("parallel"),
        name="mla_rope_tables",
    )(pos_col, inv_row, cos_mask, sin_sign)


def _mla_in_kernel(x_ref, g_ref, w_ref, gq_ref, gkv_ref, tcos_ref, tsin_ref,
                   cq_ref, ckv_ref, kr_ref, *, q_lora, kv_lora):
    h = _rms(x_ref[...], g_ref[...]).astype(BF16)
    c = _bdot(h, w_ref[...].astype(BF16))
    cq_ref[...] = _rms(c[:, :q_lora], gq_ref[...]).astype(cq_ref.dtype)
    ckv_ref[...] = _rms(c[:, q_lora:q_lora + kv_lora], gkv_ref[...]).astype(ckv_ref.dtype)
    kr = c[:, q_lora + kv_lora:]
    half = kr.shape[1] // 2
    v = jnp.concatenate([kr, jnp.zeros_like(kr)], axis=-1)
    lane = lax.broadcasted_iota(jnp.int32, v.shape, 1)
    swapped = jnp.where(lane < half, pltpu.roll(v, LANES - half, 1), pltpu.roll(v, half, 1))
    swapped = jnp.where(lane < 2 * half, swapped, 0.0)
    kr_ref[...] = v * tcos_ref[...] + swapped * tsin_ref[...]


def _mla_in(x, g, w, layer, gq, gkv, tcos, tsin):
    t, d = x.shape
    q_lora, kv_lora = gq.shape[1], gkv.shape[1]
    n_in = w.shape[2]
    tm = ROW_TILE // 2
    kern = functools.partial(_mla_in_kernel, q_lora=q_lora, kv_lora=kv_lora)
    return pl.pallas_call(
        kern,
        out_shape=(jax.ShapeDtypeStruct((t, q_lora), BF16),
                   jax.ShapeDtypeStruct((t, kv_lora), BF16),
                   jax.ShapeDtypeStruct((t, LANES), F32)),
        grid=(t // tm,),
        in_specs=[pl.BlockSpec((tm, d), lambda i: (i, 0)),
                  pl.BlockSpec((1, d), lambda i: (0, 0)),
                  pl.BlockSpec((None, d, n_in), lambda i: (layer, 0, 0)),
                  pl.BlockSpec((1, q_lora), lambda i: (0, 0)),
                  pl.BlockSpec((1, kv_lora), lambda i: (0, 0)),
                  pl.BlockSpec((tm, LANES), lambda i: (i, 0)),
                  pl.BlockSpec((tm, LANES), lambda i: (i, 0))],
        out_specs=(pl.BlockSpec((tm, q_lora), lambda i: (i, 0)),
                   pl.BlockSpec((tm, kv_lora), lambda i: (i, 0)),
                   pl.BlockSpec((tm, LANES), lambda i: (i, 0))),
        compiler_params=_params("parallel"),
        name="mla_in",
    )(x, g, w, gq, gkv, tcos, tsin)


def _mla_q_weight_kernel(w_ref, o_ref):
    half = MLA_ROPE // 2
    w = w_ref[...]
    for t in range(o_ref.shape[0]):
        o = t * MLA_QD
        nope = w[:, o:o + MLA_NOPE]
        x1 = w[:, o + MLA_NOPE:o + MLA_NOPE + half]
        x2 = w[:, o + MLA_NOPE + half:o + MLA_QD]
        o_ref[t] = jnp.concatenate([nope, x1, x2, x2, x1], axis=-1).astype(o_ref.dtype)


def _mla_q_weights(w_uq, layer):
    q_lora = w_uq.shape[1]
    heads = w_uq.shape[2] // MLA_QD
    hp = 2
    return pl.pallas_call(
        _mla_q_weight_kernel,
        out_shape=jax.ShapeDtypeStruct((heads, q_lora, MLA_PAD), BF16),
        grid=(heads // hp,),
        in_specs=[pl.BlockSpec((None, q_lora, hp * MLA_QD), lambda h: (layer, 0, h))],
        out_specs=pl.BlockSpec((hp, q_lora, MLA_PAD), lambda h: (h, 0, 0)),
        compiler_params=_params("parallel"),
        name="mla_q_weights",
    )(w_uq)


def _mla_q_kernel(cq_ref, w_ref, tcos_ref, tsin_ref, o_ref, *, scale):
    cq = cq_ref[...]
    tcos, tsin = tcos_ref[...], tsin_ref[...]
    for g in range(o_ref.shape[0]):
        res = _bdot(cq, w_ref[g])
        rot = res[:, MLA_NOPE:]
        rope = rot * tcos + pltpu.roll(rot, LANES // 2, 1) * tsin
        o_ref[g, :, :MLA_NOPE] = (res[:, :MLA_NOPE] * scale).astype(o_ref.dtype)
        o_ref[g, :, MLA_NOPE:] = (rope * scale).astype(o_ref.dtype)


def _mla_q(cq, w_q, tcos, tsin):
    t, q_lora = cq.shape
    heads = w_q.shape[0]
    tm = ROW_TILE
    hp = MLA_HEADS_PER_STEP
    kern = functools.partial(_mla_q_kernel, scale=MLA_QD ** -0.5)
    return pl.pallas_call(
        kern,
        out_shape=jax.ShapeDtypeStruct((heads, t, MLA_PAD), BF16),
        grid=(t // tm, heads // hp),
        in_specs=[pl.BlockSpec((tm, q_lora), lambda i, h: (i, 0)),
                  pl.BlockSpec((hp, q_lora, MLA_PAD), lambda i, h: (h, 0, 0)),
                  pl.BlockSpec((tm, LANES), lambda i, h: (i, 0)),
                  pl.BlockSpec((tm, LANES), lambda i, h: (i, 0))],
        out_specs=pl.BlockSpec((hp, tm, MLA_PAD), lambda i, h: (h, i, 0)),
        compiler_params=_params("parallel", "parallel"),
        name="mla_q",
    )(cq, w_q, tcos, tsin)


def _mla_kv_kernel(ckv_ref, w_ref, kr_ref, k_ref, v_ref):
    ckv = ckv_ref[...]
    per_head = MLA_NOPE + MLA_V
    for g in range(k_ref.shape[0]):
        kv = _bdot(ckv, w_ref[:, g * per_head:(g + 1) * per_head].astype(BF16))
        k_ref[g, :, :MLA_NOPE] = kv[:, :MLA_NOPE].astype(k_ref.dtype)
        k_ref[g, :, MLA_NOPE:] = kr_ref[...].astype(k_ref.dtype)
        v_ref[g, :, :MLA_V] = kv[:, MLA_NOPE:].astype(v_ref.dtype)
        v_ref[g, :, MLA_V:] = jnp.ones((v_ref.shape[1], v_ref.shape[2] - MLA_V), v_ref.dtype)


def _mla_kv(ckv, w_ukv, layer, k_rope):
    t, kv_lora = ckv.shape
    per_head = MLA_NOPE + MLA_V
    heads = w_ukv.shape[2] // per_head
    tm = ROW_TILE
    hp = MLA_HEADS_PER_STEP
    return pl.pallas_call(
        _mla_kv_kernel,
        out_shape=(jax.ShapeDtypeStruct((heads, t, MLA_PAD), BF16),
                   jax.ShapeDtypeStruct((heads, t, 2 * MLA_V), BF16)),
        grid=(t // tm, heads // hp),
        in_specs=[pl.BlockSpec((tm, kv_lora), lambda i, h: (i, 0)),
                  pl.BlockSpec((None, kv_lora, hp * per_head), lambda i, h: (layer, 0, h)),
                  pl.BlockSpec((tm, LANES), lambda i, h: (i, 0))],
        out_specs=(pl.BlockSpec((hp, tm, MLA_PAD), lambda i, h: (h, i, 0)),
                   pl.BlockSpec((hp, tm, 2 * MLA_V), lambda i, h: (h, i, 0))),
        compiler_params=_params("parallel", "parallel"),
        name="mla_kv",
    )(ckv, w_ukv, k_rope)


def _flash_kernel(q_ref, k_ref, v_ref, o_ref, sa_ref, sb_ref, m_ref, acc_ref):
    qi = pl.program_id(2)
    n_heads, tq, _ = q_ref.shape
    m_ref[...] = jnp.full_like(m_ref, NEG_BIG)
    acc_ref[...] = jnp.zeros_like(acc_ref)

    def kv_rows(ki):
        return pl.ds(pl.multiple_of(ki * tq, tq), tq)

    def scores_into(s_ref, ki):
        for g in range(n_heads):
            s_ref[g] = lax.dot_general(q_ref[g], k_ref[g, kv_rows(ki), :], (((1,), (1,)), ((), ())),
                                       preferred_element_type=F32)

    def consume(s_ref, ki, masked):
        for g in range(n_heads):
            s = s_ref[g]
            if masked:
                row = lax.broadcasted_iota(jnp.int32, s.shape, 0)
                col = lax.broadcasted_iota(jnp.int32, s.shape, 1)
                s = jnp.where(col <= row, s, NEG_BIG)
            m_prev = m_ref[g]
            m_new = jnp.maximum(m_prev, jnp.max(s, axis=-1, keepdims=True))
            alpha = jnp.exp(m_prev - m_new)
            p = jnp.exp(s - m_new).astype(BF16)
            acc_ref[g] = alpha * acc_ref[g] + _bdot(p, v_ref[g, kv_rows(ki), :])
            m_ref[g] = m_new

    scores_into(sa_ref, 0)

    def pair(j, carry):
        scores_into(sb_ref, 2 * j + 1)
        consume(sa_ref, 2 * j, False)
        scores_into(sa_ref, 2 * j + 2)
        consume(sb_ref, 2 * j + 1, False)
        return carry

    lax.fori_loop(0, qi // 2, pair, 0)

    @pl.when(qi % 2 == 1)
    def _():
        scores_into(sb_ref, qi)
        consume(sa_ref, qi - 1, False)
        consume(sb_ref, qi, True)

    @pl.when(qi % 2 == 0)
    def _():
        consume(sa_ref, qi, True)

    for g in range(n_heads):
        acc = acc_ref[g]
        o_ref[:, g * MLA_V:(g + 1) * MLA_V] = (acc[:, :MLA_V] / acc[:, MLA_V:]).astype(o_ref.dtype)


def _flash_attention(q, k, v, *, batch, seq):
    heads, t, _ = q.shape
    tile = ATTN_TILE
    nq = seq // tile
    hp = ATTN_HEADS_PER_STEP
    return pl.pallas_call(
        _flash_kernel,
        out_shape=jax.ShapeDtypeStruct((t, heads * MLA_V), BF16),
        grid=(batch, heads // hp, nq),
        in_specs=[pl.BlockSpec((hp, tile, MLA_PAD), lambda b, h, i: (h, b * nq + i, 0)),
                  pl.BlockSpec((hp, seq, MLA_PAD), lambda b, h, i: (h, b, 0)),
                  pl.BlockSpec((hp, seq, 2 * MLA_V), lambda b, h, i: (h, b, 0))],
        out_specs=pl.BlockSpec((tile, hp * MLA_V), lambda b, h, i: (b * nq + i, h)),
        scratch_shapes=[pltpu.VMEM((hp, tile, tile), F32),
                        pltpu.VMEM((hp, tile, tile), F32),
                        pltpu.VMEM((hp, tile, 1), F32),
                        pltpu.VMEM((hp, tile, 2 * MLA_V), F32)],
        compiler_params=_params("parallel", "parallel", "arbitrary"),
        name="mla_flash_attention",
    )(q, k, v)


def _retention_layer(x, g_pre, g_post, w_in, gn_g, w_out, layer, cos, sin, log_gamma, *, batch, seq):
    d = x.shape[1]
    dk, dv = RET_QK_DIM, RET_V_DIM
    heads = d // dk
    n_qk = 2 * heads * dk
    n_v = heads * dv
    n_q = heads * dk
    q = _norm_matmul_rope(x, g_pre, w_in, layer, cos, sin, col0=0, n_cols=n_q, head_dim=dk,
                          scale=None, out_dtype=BF16, name="ret_q_proj")
    k = _norm_matmul_rope(x, g_pre, w_in, layer, cos, sin, col0=n_q, n_cols=n_q, head_dim=dk,
                          scale=dk ** -0.5, out_dtype=F32, name="ret_k_proj")
    v = _norm_matmul(x, g_pre, w_in, layer, col0=n_qk, n_cols=n_v, tn=512, out_dtype=BF16,
                     name="ret_v_proj")
    gate = _norm_matmul(x, g_pre, w_in, layer, col0=n_qk + n_v, n_cols=n_v, tn=512, out_dtype=F32,
                        name="ret_gate_proj", activation=_silu)
    y = _retention_core(q, k, v, gate, gn_g, log_gamma, batch=batch, seq=seq)
    return _matmul_norm_residual(y, w_out, layer, x, g_post, tk=512, name="ret_out_proj")


def _mla_layer(x, g_pre, g_post, w_in, g_q, g_kv, w_uq, w_ukv, w_out, layer, tcos, tsin, *, batch, seq):
    cq, ckv, k_rope = _mla_in(x, g_pre, w_in, layer, g_q, g_kv, tcos, tsin)
    q = _mla_q(cq, _mla_q_weights(w_uq, layer), tcos, tsin)
    k, v = _mla_kv(ckv, w_ukv, layer, k_rope)
    o = _flash_attention(q, k, v, batch=batch, seq=seq)
    return _matmul_norm_residual(o, w_out, layer, x, g_post, tk=512, name="mla_out_proj")


def _ffn(x, g_pre, g_post, w_gu, w_down, layer):
    act = _norm_swiglu(x, g_pre, w_gu, layer, tn=512, name="ffn_gate_up")
    return _matmul_norm_residual(act, w_down, layer, x, g_post, tk=512, name="ffn_down_proj")


def kernel(x, positions, norm_mix_pre, norm_mix_post, norm_ffn_pre, norm_ffn_post, ret_w_in, ret_gn_g, ret_w_out, mla_w_in, mla_g_q, mla_g_kv, mla_w_uq, mla_w_ukv, mla_w_out, ffn_w_gu, ffn_w_down):
    batch, seq, d = x.shape
    depth = norm_mix_pre.shape[0]
    t = batch * seq
    xf = x.reshape(t, d)
    pos_col = positions.reshape(t, 1)
    cos_r, sin_r = _rope_tables(pos_col, RET_QK_DIM)
    tcos_m, tsin_m = _mla_rope_tables(pos_col)
    ret_heads = d // RET_QK_DIM
    log_gamma = jnp.log1p(-jnp.exp2(-5.0 - jnp.arange(ret_heads, dtype=F32)))

    def row(a, i):
        return a[i].reshape(1, -1)

    for i in range(depth):
        j = i // N_MIXERS
        if i % N_MIXERS == 0:
            xf = _retention_layer(xf, row(norm_mix_pre, i), row(norm_mix_post, i), ret_w_in,
                                  row(ret_gn_g, j), ret_w_out, j, cos_r, sin_r, log_gamma,
                                  batch=batch, seq=seq)
        else:
            xf = _mla_layer(xf, row(norm_mix_pre, i), row(norm_mix_post, i), mla_w_in,
                            row(mla_g_q, j), row(mla_g_kv, j), mla_w_uq, mla_w_ukv,
                            mla_w_out, j, tcos_m, tsin_m, batch=batch, seq=seq)
        xf = _ffn(xf, row(norm_ffn_pre, i), row(norm_ffn_post, i), ffn_w_gu, ffn_w_down, i)
    return xf.reshape(batch, seq, d)
```

```python
import functools

import jax
import jax.numpy as jnp
from jax import lax
from jax.experimental import pallas as pl
from jax.experimental.pallas import tpu as pltpu

F32 = jnp.float32
BF16 = jnp.bfloat16

RET_QK_DIM = 256
RET_V_DIM = 2 * RET_QK_DIM
MLA_NOPE = 128
MLA_ROPE = 64
MLA_V = 128
MLA_QD = MLA_NOPE + MLA_ROPE
MLA_PAD = 256
LANES = 128
ROPE_THETA = 10000.0
NORM_EPS = 1e-6
N_MIXERS = 2

V7X_VMEM_BYTES = 64 * 1024 * 1024
VMEM_LIMIT_BYTES = V7X_VMEM_BYTES - 8 * 1024 * 1024

ROW_TILE = 1024
PROJ_COL_TILE = 1024
FFN_COL_TILE = 512
OUT_K_TILE = 512
RET_CHUNK = 256
RET_SEQ_BLOCK = 1024
RET_HEADS_PER_STEP = 2
MLA_HEADS_PER_STEP = 4
ATTN_TILE = 512
ATTN_HEADS_PER_STEP = 2
OUT_COL_CHUNK = 512
NEG_BIG = -1e30


def _params(*sem):
    return pltpu.CompilerParams(dimension_semantics=sem, vmem_limit_bytes=VMEM_LIMIT_BYTES)


def _rms(x, g):
    return x * lax.rsqrt(jnp.mean(x * x, axis=-1, keepdims=True) + NORM_EPS) * g


def _bdot(a, b):
    return jnp.dot(a, b, preferred_element_type=F32)


def _silu(x):
    return x * jax.nn.sigmoid(x)


def _rope_table_kernel(pos_ref, inv_ref, cos_ref, sin_ref):
    ang = pos_ref[...].astype(F32) * inv_ref[...]
    cos_ref[...] = jnp.cos(ang)
    sin_ref[...] = jnp.sin(ang)


def _rope_tables(pos_col, d):
    t = pos_col.shape[0]
    f = d // 2
    inv = (ROPE_THETA ** (-jnp.arange(0, d, 2, dtype=F32) / d)).reshape(1, f)
    tm = ROW_TILE
    return pl.pallas_call(
        _rope_table_kernel,
        out_shape=(jax.ShapeDtypeStruct((t, f), F32), jax.ShapeDtypeStruct((t, f), F32)),
        grid=(t // tm,),
        in_specs=[pl.BlockSpec((tm, 1), lambda i: (i, 0)),
                  pl.BlockSpec((1, f), lambda i: (0, 0))],
        out_specs=(pl.BlockSpec((tm, f), lambda i: (i, 0)),
                   pl.BlockSpec((tm, f), lambda i: (i, 0))),
        compiler_params=_params("parallel"),
        name=f"rope_tables_{d}",
    )(pos_col, inv)


def _prenorm_kernel(x_ref, g_ref, h_ref):
    h_ref[...] = _rms(x_ref[...], g_ref[...]).astype(h_ref.dtype)


def _prenorm(x, g):
    t, d = x.shape
    tm = ROW_TILE // 2
    return pl.pallas_call(
        _prenorm_kernel,
        out_shape=jax.ShapeDtypeStruct((t, d), BF16),
        grid=(t // tm,),
        in_specs=[pl.BlockSpec((tm, d), lambda i: (i, 0)),
                  pl.BlockSpec((1, d), lambda i: (0, 0))],
        out_specs=pl.BlockSpec((tm, d), lambda i: (i, 0)),
        compiler_params=_params("parallel"),
        name="prenorm",
    )(x, g)


def _cast_weight_once(w_ref, wb_ref):
    @pl.when(pl.program_id(1) == 0)
    def _():
        wb_ref[...] = w_ref[...].astype(BF16)


def _proj_plain_kernel(h_ref, w_ref, o_ref, wb_ref, *, activation):
    _cast_weight_once(w_ref, wb_ref)
    acc = _bdot(h_ref[...], wb_ref[...])
    if activation is not None:
        acc = activation(acc)
    o_ref[...] = acc.astype(o_ref.dtype)


def _proj_rope_kernel(h_ref, w_ref, cos_ref, sin_ref, o_ref, wb_ref, *, head_dim, scale):
    _cast_weight_once(w_ref, wb_ref)
    acc = _bdot(h_ref[...], wb_ref[...])
    cos, sin = cos_ref[...], sin_ref[...]
    half = head_dim // 2
    for c0 in range(0, acc.shape[1], head_dim):
        x1, x2 = acc[:, c0:c0 + half], acc[:, c0 + half:c0 + head_dim]
        r1 = x1 * cos - x2 * sin
        r2 = x1 * sin + x2 * cos
        if scale is not None:
            r1, r2 = r1 * scale, r2 * scale
        o_ref[:, c0:c0 + half] = r1.astype(o_ref.dtype)
        o_ref[:, c0 + half:c0 + head_dim] = r2.astype(o_ref.dtype)


def _proj_swiglu_kernel(h_ref, wg_ref, wu_ref, o_ref, wgb_ref, wub_ref):
    _cast_weight_once(wg_ref, wgb_ref)
    _cast_weight_once(wu_ref, wub_ref)
    h = h_ref[...]
    gate = _bdot(h, wgb_ref[...])
    up = _bdot(h, wub_ref[...])
    o_ref[...] = (_silu(gate) * up).astype(o_ref.dtype)


def _proj(h, w, layer, *, col0, n_cols, tn, out_dtype, name, activation=None):
    t, kdim = h.shape
    tm = ROW_TILE
    tn = min(tn, n_cols)
    off = col0 // tn
    return pl.pallas_call(
        functools.partial(_proj_plain_kernel, activation=activation),
        out_shape=jax.ShapeDtypeStruct((t, n_cols), out_dtype),
        grid=(n_cols // tn, t // tm),
        in_specs=[pl.BlockSpec((tm, kdim), lambda j, i: (i, 0)),
                  pl.BlockSpec((None, kdim, tn), lambda j, i: (layer, 0, j + off))],
        out_specs=pl.BlockSpec((tm, tn), lambda j, i: (i, j)),
        scratch_shapes=[pltpu.VMEM((kdim, tn), BF16)],
        compiler_params=_params("arbitrary", "arbitrary"),
        name=name,
    )(h, w)


def _proj_rope(h, w, layer, cos, sin, *, col0, n_cols, tn, head_dim, scale, out_dtype, name):
    t, kdim = h.shape
    tm = ROW_TILE
    tn = min(tn, n_cols)
    off = col0 // tn
    kern = functools.partial(_proj_rope_kernel, head_dim=head_dim, scale=scale)
    return pl.pallas_call(
        kern,
        out_shape=jax.ShapeDtypeStruct((t, n_cols), out_dtype),
        grid=(n_cols // tn, t // tm),
        in_specs=[pl.BlockSpec((tm, kdim), lambda j, i: (i, 0)),
                  pl.BlockSpec((None, kdim, tn), lambda j, i: (layer, 0, j + off)),
                  pl.BlockSpec((tm, head_dim // 2), lambda j, i: (i, 0)),
                  pl.BlockSpec((tm, head_dim // 2), lambda j, i: (i, 0))],
        out_specs=pl.BlockSpec((tm, tn), lambda j, i: (i, j)),
        scratch_shapes=[pltpu.VMEM((kdim, tn), BF16)],
        compiler_params=_params("arbitrary", "arbitrary"),
        name=name,
    )(h, w, cos, sin)


def _proj_swiglu(h, w_gu, layer, *, tn, name):
    t, kdim = h.shape
    d_ff = w_gu.shape[2] // 2
    tm = ROW_TILE
    n_tiles = d_ff // tn
    return pl.pallas_call(
        _proj_swiglu_kernel,
        out_shape=jax.ShapeDtypeStruct((t, d_ff), BF16),
        grid=(n_tiles, t // tm),
        in_specs=[pl.BlockSpec((tm, kdim), lambda j, i: (i, 0)),
                  pl.BlockSpec((None, kdim, tn), lambda j, i: (layer, 0, j)),
                  pl.BlockSpec((None, kdim, tn), lambda j, i: (layer, 0, j + n_tiles))],
        out_specs=pl.BlockSpec((tm, tn), lambda j, i: (i, j)),
        scratch_shapes=[pltpu.VMEM((kdim, tn), BF16), pltpu.VMEM((kdim, tn), BF16)],
        compiler_params=_params("arbitrary", "arbitrary"),
        name=name,
    )(h, w_gu, w_gu)


def _cast_kernel(w_ref, o_ref):
    o_ref[...] = w_ref[...].astype(o_ref.dtype)


def _cast_bf16(w, *, tk):
    n_layers, kdim, n = w.shape
    return pl.pallas_call(
        _cast_kernel,
        out_shape=jax.ShapeDtypeStruct(w.shape, BF16),
        grid=(n_layers, kdim // tk),
        in_specs=[pl.BlockSpec((None, tk, n), lambda l, k: (l, k, 0))],
        out_specs=pl.BlockSpec((None, tk, n), lambda l, k: (l, k, 0)),
        compiler_params=_params("parallel", "parallel"),
        name="cast_weight_bf16",
    )(w)


def _out_proj_kernel(a_ref, w_ref, x_ref, g_ref, *rest, n_k, emit_next):
    if emit_next:
        gn_ref, o_ref, h_ref = rest
    else:
        (o_ref,) = rest
    k = pl.program_id(1)
    a = a_ref[...]

    def accumulate(first):
        for n in range(0, o_ref.shape[1], OUT_COL_CHUNK):
            cols = slice(n, n + OUT_COL_CHUNK)
            part = _bdot(a, w_ref[:, cols])
            if first:
                o_ref[:, cols] = part
            else:
                o_ref[:, cols] += part

    @pl.when(k == 0)
    def _():
        accumulate(True)

    @pl.when(k > 0)
    def _():
        accumulate(False)

    @pl.when(k == n_k - 1)
    def _():
        x_new = x_ref[...] + _rms(o_ref[...], g_ref[...])
        o_ref[...] = x_new
        if emit_next:
            h_ref[...] = _rms(x_new, gn_ref[...]).astype(h_ref.dtype)


def _out_proj(a, w_bf16, layer, x, g_post, g_next, *, tk, name):
    t, kdim = a.shape
    d = w_bf16.shape[2]
    tm = ROW_TILE
    n_k = kdim // tk
    emit_next = g_next is not None
    kern = functools.partial(_out_proj_kernel, n_k=n_k, emit_next=emit_next)
    row_spec = pl.BlockSpec((tm, d), lambda i, k: (i, 0))
    gain_spec = pl.BlockSpec((1, d), lambda i, k: (0, 0))
    operands = [a, w_bf16, x, g_post]
    in_specs = [pl.BlockSpec((tm, tk), lambda i, k: (i, k)),
                pl.BlockSpec((None, tk, d), lambda i, k: (layer, k, 0)),
                row_spec, gain_spec]
    out_shape = [jax.ShapeDtypeStruct((t, d), F32)]
    out_specs = [row_spec]
    if emit_next:
        operands.append(g_next)
        in_specs.append(gain_spec)
        out_shape.append(jax.ShapeDtypeStruct((t, d), BF16))
        out_specs.append(row_spec)
    res = pl.pallas_call(
        kern,
        out_shape=tuple(out_shape),
        grid=(t // tm, n_k),
        in_specs=in_specs,
        out_specs=tuple(out_specs),
        compiler_params=_params("parallel", "arbitrary"),
        name=name,
    )(*operands)
    return (res[0], res[1]) if emit_next else (res[0], None)


def _retention_kernel(lg_ref, q_ref, k_ref, v_ref, g_ref, gn_ref, o_ref, state_ref, *, heads_per_step):
    c_len = RET_CHUNK
    dk, dv = RET_QK_DIM, RET_V_DIM
    n_chunks = q_ref.shape[0] // c_len

    @pl.when(pl.program_id(2) == 0)
    def _():
        state_ref[...] = jnp.zeros_like(state_ref)

    row = lax.broadcasted_iota(jnp.int32, (c_len, c_len), 0)
    col = lax.broadcasted_iota(jnp.int32, (c_len, c_len), 1)
    rel = (row - col).astype(F32)
    causal = rel >= 0
    rel_pos = jnp.where(causal, rel, 0.0)
    idx = lax.broadcasted_iota(jnp.int32, (c_len, 1), 0).astype(F32)
    decays = []
    for g in range(heads_per_step):
        lg = jnp.full((1, 1), lg_ref[pl.program_id(1) * heads_per_step + g], F32)
        decays.append((jnp.where(causal, jnp.exp(lg * rel_pos), 0.0),
                       jnp.exp(lg * (idx + 1.0)),
                       jnp.exp(lg * (c_len - 1.0 - idx)),
                       jnp.exp(lg * c_len)))

    for c in range(n_chunks):
        rows = slice(c * c_len, (c + 1) * c_len)
        for g in range(heads_per_step):
            decay_in, decay_q, decay_k, decay_chunk = decays[g]
            qk_cols = slice(g * dk, (g + 1) * dk)
            v_cols = slice(g * dv, (g + 1) * dv)
            q = q_ref[rows, qk_cols]
            k = k_ref[rows, qk_cols]
            v = v_ref[rows, v_cols]
            scores = lax.dot_general(q, k.astype(BF16), (((1,), (1,)), ((), ())),
                                     preferred_element_type=F32) * decay_in
            state = state_ref[g]
            y = _bdot(scores.astype(BF16), v) + _bdot(q, state.astype(BF16)) * decay_q
            kd_t = (k * decay_k).T.astype(BF16)
            state_ref[g] = state * decay_chunk + _bdot(kd_t, v)

            mu = jnp.mean(y, axis=-1, keepdims=True)
            yc = y - mu
            var = jnp.mean(yc * yc, axis=-1, keepdims=True)
            yn = yc * lax.rsqrt(var + NORM_EPS) * gn_ref[:, v_cols]
            o_ref[rows, v_cols] = (g_ref[rows, v_cols] * yn).astype(o_ref.dtype)


def _retention_core(q, k, v, gate, gn_g, log_gamma, *, batch, seq):
    t = q.shape[0]
    dk, dv = RET_QK_DIM, RET_V_DIM
    heads = v.shape[1] // dv
    hp = RET_HEADS_PER_STEP
    lb = RET_SEQ_BLOCK
    nb = seq // lb
    kern = functools.partial(_retention_kernel, heads_per_step=hp)
    return pl.pallas_call(
        kern,
        out_shape=jax.ShapeDtypeStruct((t, heads * dv), BF16),
        grid=(batch, heads // hp, nb),
        in_specs=[pl.BlockSpec(memory_space=pltpu.SMEM),
                  pl.BlockSpec((lb, hp * dk), lambda b, h, s: (b * nb + s, h)),
                  pl.BlockSpec((lb, hp * dk), lambda b, h, s: (b * nb + s, h)),
                  pl.BlockSpec((lb, hp * dv), lambda b, h, s: (b * nb + s, h)),
                  pl.BlockSpec((lb, hp * dv), lambda b, h, s: (b * nb + s, h)),
                  pl.BlockSpec((1, hp * dv), lambda b, h, s: (0, h))],
        out_specs=pl.BlockSpec((lb, hp * dv), lambda b, h, s: (b * nb + s, h)),
        scratch_shapes=[pltpu.VMEM((hp, dk, dv), F32)],
        compiler_params=_params("parallel", "parallel", "arbitrary"),
        name="retention_core",
    )(log_gamma, q, k, v, gate, gn_g)


def _mla_rope_table_kernel(pos_ref, inv_ref, cos_mask_ref, sin_sign_ref, tcos_ref, tsin_ref):
    ang = pos_ref[...].astype(F32) * inv_ref[...]
    tcos_ref[...] = jnp.cos(ang) * cos_mask_ref[...]
    tsin_ref[...] = jnp.sin(ang) * sin_sign_ref[...]


def _mla_rope_tables(pos_col):
    t = pos_col.shape[0]
    half = MLA_ROPE // 2
    inv = ROPE_THETA ** (-jnp.arange(0, MLA_ROPE, 2, dtype=F32) / MLA_ROPE)
    ones, zeros = jnp.ones((half,), F32), jnp.zeros((half,), F32)
    inv_row = jnp.concatenate([inv, inv, zeros, zeros]).reshape(1, LANES)
    cos_mask = jnp.concatenate([ones, ones, zeros, zeros]).reshape(1, LANES)
    sin_sign = jnp.concatenate([-ones, ones, zeros, zeros]).reshape(1, LANES)
    tm = ROW_TILE
    row = pl.BlockSpec((1, LANES), lambda i: (0, 0))
    return pl.pallas_call(
        _mla_rope_table_kernel,
        out_shape=(jax.ShapeDtypeStruct((t, LANES), F32), jax.ShapeDtypeStruct((t, LANES), F32)),
        grid=(t // tm,),
        in_specs=[pl.BlockSpec((tm, 1), lambda i: (i, 0)), row, row, row],
        out_specs=(pl.BlockSpec((tm, LANES), lambda i: (i, 0)),
                   pl.BlockSpec((tm, LANES), lambda i: (i, 0))),
        compiler_params=_params("parallel"),
        name="mla_rope_tables",
    )(pos_col, inv_row, cos_mask, sin_sign)


def _mla_in_kernel(h_ref, w_ref, gq_ref, gkv_ref, tcos_ref, tsin_ref,
                   cq_ref, ckv_ref, kr_ref, wb_ref, *, q_lora, kv_lora):
    @pl.when(pl.program_id(0) == 0)
    def _():
        wb_ref[...] = w_ref[...].astype(BF16)

    c = _bdot(h_ref[...], wb_ref[...])
    cq_ref[...] = _rms(c[:, :q_lora], gq_ref[...]).astype(cq_ref.dtype)
    ckv_ref[...] = _rms(c[:, q_lora:q_lora + kv_lora], gkv_ref[...]).astype(ckv_ref.dtype)
    kr = c[:, q_lora + kv_lora:]
    half = kr.shape[1] // 2
    v = jnp.concatenate([kr, jnp.zeros_like(kr)], axis=-1)
    lane = lax.broadcasted_iota(jnp.int32, v.shape, 1)
    swapped = jnp.where(lane < half, pltpu.roll(v, LANES - half, 1), pltpu.roll(v, half, 1))
    swapped = jnp.where(lane < 2 * half, swapped, 0.0)
    kr_ref[...] = v * tcos_ref[...] + swapped * tsin_ref[...]


def _mla_in(h, w, layer, gq, gkv, tcos, tsin):
    t, d = h.shape
    q_lora, kv_lora = gq.shape[1], gkv.shape[1]
    n_in = w.shape[2]
    tm = ROW_TILE // 2
    kern = functools.partial(_mla_in_kernel, q_lora=q_lora, kv_lora=kv_lora)
    return pl.pallas_call(
        kern,
        out_shape=(jax.ShapeDtypeStruct((t, q_lora), BF16),
                   jax.ShapeDtypeStruct((t, kv_lora), BF16),
                   jax.ShapeDtypeStruct((t, LANES), F32)),
        grid=(t // tm,),
        in_specs=[pl.BlockSpec((tm, d), lambda i: (i, 0)),
                  pl.BlockSpec((None, d, n_in), lambda i: (layer, 0, 0)),
                  pl.BlockSpec((1, q_lora), lambda i: (0, 0)),
                  pl.BlockSpec((1, kv_lora), lambda i: (0, 0)),
                  pl.BlockSpec((tm, LANES), lambda i: (i, 0)),
                  pl.BlockSpec((tm, LANES), lambda i: (i, 0))],
        out_specs=(pl.BlockSpec((tm, q_lora), lambda i: (i, 0)),
                   pl.BlockSpec((tm, kv_lora), lambda i: (i, 0)),
                   pl.BlockSpec((tm, LANES), lambda i: (i, 0))),
        scratch_shapes=[pltpu.VMEM((d, n_in), BF16)],
        compiler_params=_params("arbitrary"),
        name="mla_in",
    )(h, w, gq, gkv, tcos, tsin)


def _mla_q_weight_kernel(w_ref, o_ref):
    half = MLA_ROPE // 2
    w = w_ref[...]
    for t in range(o_ref.shape[0]):
        o = t * MLA_QD
        nope = w[:, o:o + MLA_NOPE]
        x1 = w[:, o + MLA_NOPE:o + MLA_NOPE + half]
        x2 = w[:, o + MLA_NOPE + half:o + MLA_QD]
        o_ref[t] = jnp.concatenate([nope, x1, x2, x2, x1], axis=-1).astype(o_ref.dtype)


def _mla_q_weights(w_uq, layer):
    q_lora = w_uq.shape[1]
    heads = w_uq.shape[2] // MLA_QD
    hp = 2
    return pl.pallas_call(
        _mla_q_weight_kernel,
        out_shape=jax.ShapeDtypeStruct((heads, q_lora, MLA_PAD), BF16),
        grid=(heads // hp,),
        in_specs=[pl.BlockSpec((None, q_lora, hp * MLA_QD), lambda h: (layer, 0, h))],
        out_specs=pl.BlockSpec((hp, q_lora, MLA_PAD), lambda h: (h, 0, 0)),
        compiler_params=_params("parallel"),
        name="mla_q_weights",
    )(w_uq)


def _mla_q_kernel(cq_ref, w_ref, tcos_ref, tsin_ref, o_ref, *, scale):
    cq = cq_ref[...]
    tcos, tsin = tcos_ref[...], tsin_ref[...]
    for g in range(o_ref.shape[0]):
        res = _bdot(cq, w_ref[g])
        rot = res[:, MLA_NOPE:]
        rope = rot * tcos + pltpu.roll(rot, LANES // 2, 1) * tsin
        o_ref[g, :, :MLA_NOPE] = (res[:, :MLA_NOPE] * scale).astype(o_ref.dtype)
        o_ref[g, :, MLA_NOPE:] = (rope * scale).astype(o_ref.dtype)


def _mla_q(cq, w_q, tcos, tsin):
    t, q_lora = cq.shape
    heads = w_q.shape[0]
    tm = ROW_TILE
    hp = MLA_HEADS_PER_STEP
    kern = functools.partial(_mla_q_kernel, scale=MLA_QD ** -0.5)
    return pl.pallas_call(
        kern,
        out_shape=jax.ShapeDtypeStruct((heads, t, MLA_PAD), BF16),
        grid=(t // tm, heads // hp),
        in_specs=[pl.BlockSpec((tm, q_lora), lambda i, h: (i, 0)),
                  pl.BlockSpec((hp, q_lora, MLA_PAD), lambda i, h: (h, 0, 0)),
                  pl.BlockSpec((tm, LANES), lambda i, h: (i, 0)),
                  pl.BlockSpec((tm, LANES), lambda i, h: (i, 0))],
        out_specs=pl.BlockSpec((hp, tm, MLA_PAD), lambda i, h: (h, i, 0)),
        compiler_params=_params("parallel", "parallel"),
        name="mla_q",
    )(cq, w_q, tcos, tsin)


def _mla_kv_kernel(ckv_ref, w_ref, kr_ref, k_ref, v_ref):
    ckv = ckv_ref[...]
    per_head = MLA_NOPE + MLA_V
    for g in range(k_ref.shape[0]):
        kv = _bdot(ckv, w_ref[:, g * per_head:(g + 1) * per_head].astype(BF16))
        k_ref[g, :, :MLA_NOPE] = kv[:, :MLA_NOPE].astype(k_ref.dtype)
        k_ref[g, :, MLA_NOPE:] = kr_ref[...].astype(k_ref.dtype)
        v_ref[g, :, :MLA_V] = kv[:, MLA_NOPE:].astype(v_ref.dtype)
        v_ref[g, :, MLA_V:] = jnp.ones((v_ref.shape[1], v_ref.shape[2] - MLA_V), v_ref.dtype)


def _mla_kv(ckv, w_ukv, layer, k_rope):
    t, kv_lora = ckv.shape
    per_head = MLA_NOPE + MLA_V
    heads = w_ukv.shape[2] // per_head
    tm = ROW_TILE
    hp = MLA_HEADS_PER_STEP
    return pl.pallas_call(
        _mla_kv_kernel,
        out_shape=(jax.ShapeDtypeStruct((heads, t, MLA_PAD), BF16),
                   jax.ShapeDtypeStruct((heads, t, 2 * MLA_V), BF16)),
        grid=(t // tm, heads // hp),
        in_specs=[pl.BlockSpec((tm, kv_lora), lambda i, h: (i, 0)),
                  pl.BlockSpec((None, kv_lora, hp * per_head), lambda i, h: (layer, 0, h)),
                  pl.BlockSpec((tm, LANES), lambda i, h: (i, 0))],
        out_specs=(pl.BlockSpec((hp, tm, MLA_PAD), lambda i, h: (h, i, 0)),
                   pl.BlockSpec((hp, tm, 2 * MLA_V), lambda i, h: (h, i, 0))),
        compiler_params=_params("parallel", "parallel"),
        name="mla_kv",
    )(ckv, w_ukv, k_rope)


def _flash_kernel(q_ref, k_ref, v_ref, o_ref, sa_ref, sb_ref, m_ref, acc_ref):
    qi = pl.program_id(2)
    n_heads, tq, _ = q_ref.shape
    m_ref[...] = jnp.full_like(m_ref, NEG_BIG)
    acc_ref[...] = jnp.zeros_like(acc_ref)

    def kv_rows(ki):
        return pl.ds(pl.multiple_of(ki * tq, tq), tq)

    def scores_into(s_ref, ki):
        for g in range(n_heads):
            s_ref[g] = lax.dot_general(q_ref[g], k_ref[g, kv_rows(ki), :], (((1,), (1,)), ((), ())),
                                       preferred_element_type=F32)

    def consume(s_ref, ki, masked):
        for g in range(n_heads):
            s = s_ref[g]
            if masked:
                row = lax.broadcasted_iota(jnp.int32, s.shape, 0)
                col = lax.broadcasted_iota(jnp.int32, s.shape, 1)
                s = jnp.where(col <= row, s, NEG_BIG)
            m_prev = m_ref[g]
            m_new = jnp.maximum(m_prev, jnp.max(s, axis=-1, keepdims=True))
            alpha = jnp.exp(m_prev - m_new)
            p = jnp.exp(s - m_new).astype(BF16)
            acc_ref[g] = alpha * acc_ref[g] + _bdot(p, v_ref[g, kv_rows(ki), :])
            m_ref[g] = m_new

    scores_into(sa_ref, 0)

    def pair(j, carry):
        scores_into(sb_ref, 2 * j + 1)
        consume(sa_ref, 2 * j, False)
        scores_into(sa_ref, 2 * j + 2)
        consume(sb_ref, 2 * j + 1, False)
        return carry

    lax.fori_loop(0, qi // 2, pair, 0)

    @pl.when(qi % 2 == 1)
    def _():
        scores_into(sb_ref, qi)
        consume(sa_ref, qi - 1, False)
        consume(sb_ref, qi, True)

    @pl.when(qi % 2 == 0)
    def _():
        consume(sa_ref, qi, True)

    for g in range(n_heads):
        acc = acc_ref[g]
        o_ref[:, g * MLA_V:(g + 1) * MLA_V] = (acc[:, :MLA_V] / acc[:, MLA_V:]).astype(o_ref.dtype)


def _flash_attention(q, k, v, *, batch, seq):
    heads, t, _ = q.shape
    tile = ATTN_TILE
    nq = seq // tile
    hp = ATTN_HEADS_PER_STEP
    return pl.pallas_call(
        _flash_kernel,
        out_shape=jax.ShapeDtypeStruct((t, heads * MLA_V), BF16),
        grid=(batch, heads // hp, nq),
        in_specs=[pl.BlockSpec((hp, tile, MLA_PAD), lambda b, h, i: (h, b * nq + i, 0)),
                  pl.BlockSpec((hp, seq, MLA_PAD), lambda b, h, i: (h, b, 0)),
                  pl.BlockSpec((hp, seq, 2 * MLA_V), lambda b, h, i: (h, b, 0))],
        out_specs=pl.BlockSpec((tile, hp * MLA_V), lambda b, h, i: (b * nq + i, h)),
        scratch_shapes=[pltpu.VMEM((hp, tile, tile), F32),
                        pltpu.VMEM((hp, tile, tile), F32),
                        pltpu.VMEM((hp, tile, 1), F32),
                        pltpu.VMEM((hp, tile, 2 * MLA_V), F32)],
        compiler_params=_params("parallel", "parallel", "arbitrary"),
        name="mla_flash_attention",
    )(q, k, v)


def _retention_layer(x, h, g_post, g_next, w_in, gn_g, w_out_bf16, layer, cos, sin, log_gamma, *, batch, seq):
    d = x.shape[1]
    dk, dv = RET_QK_DIM, RET_V_DIM
    heads = d // dk
    n_q = heads * dk
    n_v = heads * dv
    tn = PROJ_COL_TILE
    q = _proj_rope(h, w_in, layer, cos, sin, col0=0, n_cols=n_q, tn=tn, head_dim=dk,
                   scale=None, out_dtype=BF16, name="ret_q_proj")
    k = _proj_rope(h, w_in, layer, cos, sin, col0=n_q, n_cols=n_q, tn=tn, head_dim=dk,
                   scale=dk ** -0.5, out_dtype=F32, name="ret_k_proj")
    v = _proj(h, w_in, layer, col0=2 * n_q, n_cols=n_v, tn=tn, out_dtype=BF16, name="ret_v_proj")
    gate = _proj(h, w_in, layer, col0=2 * n_q + n_v, n_cols=n_v, tn=tn, out_dtype=F32,
                 name="ret_gate_proj", activation=_silu)
    y = _retention_core(q, k, v, gate, gn_g, log_gamma, batch=batch, seq=seq)
    return _out_proj(y, w_out_bf16, layer, x, g_post, g_next, tk=OUT_K_TILE, name="ret_out_proj")


def _mla_layer(x, h, g_post, g_next, w_in, g_q, g_kv, w_uq, w_ukv, w_out_bf16, layer, tcos, tsin, *, batch, seq):
    cq, ckv, k_rope = _mla_in(h, w_in, layer, g_q, g_kv, tcos, tsin)
    q = _mla_q(cq, _mla_q_weights(w_uq, layer), tcos, tsin)
    k, v = _mla_kv(ckv, w_ukv, layer, k_rope)
    o = _flash_attention(q, k, v, batch=batch, seq=seq)
    return _out_proj(o, w_out_bf16, layer, x, g_post, g_next, tk=OUT_K_TILE, name="mla_out_proj")


def _ffn(x, h, g_post, g_next, w_gu, w_down_bf16, layer):
    act = _proj_swiglu(h, w_gu, layer, tn=FFN_COL_TILE, name="ffn_gate_up")
    return _out_proj(act, w_down_bf16, layer, x, g_post, g_next, tk=OUT_K_TILE, name="ffn_down_proj")


def kernel(x, positions, norm_mix_pre, norm_mix_post, norm_ffn_pre, norm_ffn_post, ret_w_in, ret_gn_g, ret_w_out, mla_w_in, mla_g_q, mla_g_kv, mla_w_uq, mla_w_ukv, mla_w_out, ffn_w_gu, ffn_w_down):
    batch, seq, d = x.shape
    depth = norm_mix_pre.shape[0]
    t = batch * seq
    xf = x.reshape(t, d)
    pos_col = positions.reshape(t, 1)
    cos_r, sin_r = _rope_tables(pos_col, RET_QK_DIM)
    tcos_m, tsin_m = _mla_rope_tables(pos_col)
    ret_heads = d // RET_QK_DIM
    log_gamma = jnp.log1p(-jnp.exp2(-5.0 - jnp.arange(ret_heads, dtype=F32)))
    ret_w_out_b = _cast_bf16(ret_w_out, tk=OUT_K_TILE)
    mla_w_out_b = _cast_bf16(mla_w_out, tk=OUT_K_TILE)
    ffn_w_down_b = _cast_bf16(ffn_w_down, tk=OUT_K_TILE)

    def row(a, i):
        return a[i].reshape(1, -1)

    h = _prenorm(xf, row(norm_mix_pre, 0))
    for i in range(depth):
        j = i // N_MIXERS
        g_ffn_pre = row(norm_ffn_pre, i)
        g_next_mix = row(norm_mix_pre, i + 1) if i + 1 < depth else None
        if i % N_MIXERS == 0:
            xf, h = _retention_layer(xf, h, row(norm_mix_post, i), g_ffn_pre, ret_w_in, row(ret_gn_g, j),
                                     ret_w_out_b, j, cos_r, sin_r, log_gamma, batch=batch, seq=seq)
        else:
            xf, h = _mla_layer(xf, h, row(norm_mix_post, i), g_ffn_pre, mla_w_in, row(mla_g_q, j),
                               row(mla_g_kv, j), mla_w_uq, mla_w_ukv, mla_w_out_b, j, tcos_m, tsin_m,
                               batch=batch, seq=seq)
        xf, h = _ffn(xf, h, row(norm_ffn_post, i), g_next_mix, ffn_w_gu, ffn_w_down_b, i)
    return xf.reshape(batch, seq, d)
```

```python
import functools

import jax
import jax.numpy as jnp
from jax import lax
from jax.experimental import pallas as pl
from jax.experimental.pallas import tpu as pltpu

F32 = jnp.float32
BF16 = jnp.bfloat16

RET_QK_DIM = 256
RET_V_DIM = 2 * RET_QK_DIM
MLA_NOPE = 128
MLA_ROPE = 64
MLA_V = 128
MLA_QD = MLA_NOPE + MLA_ROPE
MLA_PAD = 256
LANES = 128
ROPE_THETA = 10000.0
NORM_EPS = 1e-6
N_MIXERS = 2

V7X_VMEM_BYTES = 64 * 1024 * 1024
VMEM_LIMIT_BYTES = V7X_VMEM_BYTES - 8 * 1024 * 1024

ROW_TILE = 1024
PROJ_COL_TILE = 1024
FFN_COL_TILE = 512
OUT_K_TILE = 512
RESID_FETCH_STEP = 2
RET_CHUNK = 256
RET_SEQ_BLOCK = 1024
RET_HEADS_PER_STEP = 2
MLA_HEADS_PER_STEP = 4
ATTN_TILE = 512
ATTN_HEADS_PER_STEP = 2
ATTN_STEPS_PER_REGION = 2
OUT_COL_CHUNK = 512
NEG_BIG = -1e30


def _params(*sem):
    return pltpu.CompilerParams(dimension_semantics=sem, vmem_limit_bytes=VMEM_LIMIT_BYTES)


def _rms(x, g):
    return x * lax.rsqrt(jnp.mean(x * x, axis=-1, keepdims=True) + NORM_EPS) * g


def _bdot(a, b):
    return jnp.dot(a, b, preferred_element_type=F32)


def _silu(x):
    return x * jax.nn.sigmoid(x)


def _rope_table_kernel(pos_ref, inv_ref, cos_ref, sin_ref):
    ang = pos_ref[...].astype(F32) * inv_ref[...]
    cos_ref[...] = jnp.cos(ang)
    sin_ref[...] = jnp.sin(ang)


def _rope_tables(pos_col, d):
    t = pos_col.shape[0]
    f = d // 2
    inv = (ROPE_THETA ** (-jnp.arange(0, d, 2, dtype=F32) / d)).reshape(1, f)
    tm = ROW_TILE
    return pl.pallas_call(
        _rope_table_kernel,
        out_shape=(jax.ShapeDtypeStruct((t, f), F32), jax.ShapeDtypeStruct((t, f), F32)),
        grid=(t // tm,),
        in_specs=[pl.BlockSpec((tm, 1), lambda i: (i, 0)),
                  pl.BlockSpec((1, f), lambda i: (0, 0))],
        out_specs=(pl.BlockSpec((tm, f), lambda i: (i, 0)),
                   pl.BlockSpec((tm, f), lambda i: (i, 0))),
        compiler_params=_params("parallel"),
        name=f"rope_tables_{d}",
    )(pos_col, inv)


def _prenorm_kernel(x_ref, g_ref, h_ref):
    h_ref[...] = _rms(x_ref[...], g_ref[...]).astype(h_ref.dtype)


def _prenorm(x, g):
    t, d = x.shape
    tm = ROW_TILE // 2
    return pl.pallas_call(
        _prenorm_kernel,
        out_shape=jax.ShapeDtypeStruct((t, d), BF16),
        grid=(t // tm,),
        in_specs=[pl.BlockSpec((tm, d), lambda i: (i, 0)),
                  pl.BlockSpec((1, d), lambda i: (0, 0))],
        out_specs=pl.BlockSpec((tm, d), lambda i: (i, 0)),
        compiler_params=_params("parallel"),
        name="prenorm",
    )(x, g)


def _cast_weight_once(w_ref, wb_ref):
    @pl.when(pl.program_id(1) == 0)
    def _():
        wb_ref[...] = w_ref[...].astype(BF16)


def _proj_plain_kernel(h_ref, w_ref, o_ref, wb_ref, *, activation):
    _cast_weight_once(w_ref, wb_ref)
    acc = _bdot(h_ref[...], wb_ref[...])
    if activation is not None:
        acc = activation(acc)
    o_ref[...] = acc.astype(o_ref.dtype)


def _proj_rope_kernel(h_ref, w_ref, cos_ref, sin_ref, o_ref, wb_ref, *, head_dim, scale):
    _cast_weight_once(w_ref, wb_ref)
    acc = _bdot(h_ref[...], wb_ref[...])
    cos, sin = cos_ref[...], sin_ref[...]
    half = head_dim // 2
    for c0 in range(0, acc.shape[1], head_dim):
        x1, x2 = acc[:, c0:c0 + half], acc[:, c0 + half:c0 + head_dim]
        r1 = x1 * cos - x2 * sin
        r2 = x1 * sin + x2 * cos
        if scale is not None:
            r1, r2 = r1 * scale, r2 * scale
        o_ref[:, c0:c0 + half] = r1.astype(o_ref.dtype)
        o_ref[:, c0 + half:c0 + head_dim] = r2.astype(o_ref.dtype)


def _proj_swiglu_kernel(h_ref, wg_ref, wu_ref, o_ref, wgb_ref, wub_ref):
    _cast_weight_once(wg_ref, wgb_ref)
    _cast_weight_once(wu_ref, wub_ref)
    h = h_ref[...]
    gate = _bdot(h, wgb_ref[...])
    up = _bdot(h, wub_ref[...])
    o_ref[...] = (_silu(gate) * up).astype(o_ref.dtype)


def _proj(h, w, layer, *, col0, n_cols, tn, out_dtype, name, activation=None):
    t, kdim = h.shape
    tm = ROW_TILE
    tn = min(tn, n_cols)
    off = col0 // tn
    return pl.pallas_call(
        functools.partial(_proj_plain_kernel, activation=activation),
        out_shape=jax.ShapeDtypeStruct((t, n_cols), out_dtype),
        grid=(n_cols // tn, t // tm),
        in_specs=[pl.BlockSpec((tm, kdim), lambda j, i: (i, 0)),
                  pl.BlockSpec((None, kdim, tn), lambda j, i: (layer, 0, j + off))],
        out_specs=pl.BlockSpec((tm, tn), lambda j, i: (i, j)),
        scratch_shapes=[pltpu.VMEM((kdim, tn), BF16)],
        compiler_params=_params("arbitrary", "arbitrary"),
        name=name,
    )(h, w)


def _proj_rope(h, w, layer, cos, sin, *, col0, n_cols, tn, head_dim, scale, out_dtype, name):
    t, kdim = h.shape
    tm = ROW_TILE
    tn = min(tn, n_cols)
    off = col0 // tn
    kern = functools.partial(_proj_rope_kernel, head_dim=head_dim, scale=scale)
    return pl.pallas_call(
        kern,
        out_shape=jax.ShapeDtypeStruct((t, n_cols), out_dtype),
        grid=(n_cols // tn, t // tm),
        in_specs=[pl.BlockSpec((tm, kdim), lambda j, i: (i, 0)),
                  pl.BlockSpec((None, kdim, tn), lambda j, i: (layer, 0, j + off)),
                  pl.BlockSpec((tm, head_dim // 2), lambda j, i: (i, 0)),
                  pl.BlockSpec((tm, head_dim // 2), lambda j, i: (i, 0))],
        out_specs=pl.BlockSpec((tm, tn), lambda j, i: (i, j)),
        scratch_shapes=[pltpu.VMEM((kdim, tn), BF16)],
        compiler_params=_params("arbitrary", "arbitrary"),
        name=name,
    )(h, w, cos, sin)


def _proj_swiglu(h, w_gu, layer, *, tn, name):
    t, kdim = h.shape
    d_ff = w_gu.shape[2] // 2
    tm = ROW_TILE
    n_tiles = d_ff // tn
    return pl.pallas_call(
        _proj_swiglu_kernel,
        out_shape=jax.ShapeDtypeStruct((t, d_ff), BF16),
        grid=(n_tiles, t // tm),
        in_specs=[pl.BlockSpec((tm, kdim), lambda j, i: (i, 0)),
                  pl.BlockSpec((None, kdim, tn), lambda j, i: (layer, 0, j)),
                  pl.BlockSpec((None, kdim, tn), lambda j, i: (layer, 0, j + n_tiles))],
        out_specs=pl.BlockSpec((tm, tn), lambda j, i: (i, j)),
        scratch_shapes=[pltpu.VMEM((kdim, tn), BF16), pltpu.VMEM((kdim, tn), BF16)],
        compiler_params=_params("arbitrary", "arbitrary"),
        name=name,
    )(h, w_gu, w_gu)


def _out_proj_kernel(a_ref, w_ref, x_ref, g_ref, *rest, n_k, emit_next):
    if emit_next:
        gn_ref, o_ref, h_ref = rest
    else:
        (o_ref,) = rest
    k = pl.program_id(1)
    a = a_ref[...]

    def accumulate(first):
        for n in range(0, o_ref.shape[1], OUT_COL_CHUNK):
            cols = slice(n, n + OUT_COL_CHUNK)
            part = _bdot(a, w_ref[:, cols].astype(BF16))
            if first:
                o_ref[:, cols] = part
            else:
                o_ref[:, cols] += part

    @pl.when(k == 0)
    def _():
        accumulate(True)

    @pl.when(k > 0)
    def _():
        accumulate(False)

    @pl.when(k == n_k - 1)
    def _():
        x_new = x_ref[...] + _rms(o_ref[...], g_ref[...])
        o_ref[...] = x_new
        if emit_next:
            h_ref[...] = _rms(x_new, gn_ref[...]).astype(h_ref.dtype)


def _out_proj(a, w, layer, x, g_post, g_next, *, tk, name):
    t, kdim = a.shape
    d = w.shape[2]
    tm = ROW_TILE
    n_k = kdim // tk
    emit_next = g_next is not None
    kern = functools.partial(_out_proj_kernel, n_k=n_k, emit_next=emit_next)
    row_spec = pl.BlockSpec((tm, d), lambda i, k: (i, 0))
    fetch_step = min(RESID_FETCH_STEP, n_k - 1)
    resid_spec = pl.BlockSpec((tm, d), lambda i, k: (jnp.where(k >= fetch_step, i, jnp.maximum(i - 1, 0)), 0))
    gain_spec = pl.BlockSpec((1, d), lambda i, k: (0, 0))
    operands = [a, w, x, g_post]
    in_specs = [pl.BlockSpec((tm, tk), lambda i, k: (i, k)),
                pl.BlockSpec((None, tk, d), lambda i, k: (layer, k, 0)),
                resid_spec, gain_spec]
    out_shape = [jax.ShapeDtypeStruct((t, d), F32)]
    out_specs = [row_spec]
    if emit_next:
        operands.append(g_next)
        in_specs.append(gain_spec)
        out_shape.append(jax.ShapeDtypeStruct((t, d), BF16))
        out_specs.append(row_spec)
    res = pl.pallas_call(
        kern,
        out_shape=tuple(out_shape),
        grid=(t // tm, n_k),
        in_specs=in_specs,
        out_specs=tuple(out_specs),
        compiler_params=_params("parallel", "arbitrary"),
        name=name,
    )(*operands)
    return (res[0], res[1]) if emit_next else (res[0], None)


def _retention_kernel(lg_ref, q_ref, k_ref, v_ref, g_ref, gn_ref, o_ref, state_ref, *, heads_per_step):
    c_len = RET_CHUNK
    dk, dv = RET_QK_DIM, RET_V_DIM
    n_chunks = q_ref.shape[0] // c_len

    @pl.when(pl.program_id(2) == 0)
    def _():
        state_ref[...] = jnp.zeros_like(state_ref)

    row = lax.broadcasted_iota(jnp.int32, (c_len, c_len), 0)
    col = lax.broadcasted_iota(jnp.int32, (c_len, c_len), 1)
    rel = (row - col).astype(F32)
    causal = rel >= 0
    rel_pos = jnp.where(causal, rel, 0.0)
    idx = lax.broadcasted_iota(jnp.int32, (c_len, 1), 0).astype(F32)
    decays = []
    for g in range(heads_per_step):
        lg = jnp.full((1, 1), lg_ref[pl.program_id(1) * heads_per_step + g], F32)
        decays.append((jnp.where(causal, jnp.exp(lg * rel_pos), 0.0),
                       jnp.exp(lg * (idx + 1.0)),
                       jnp.exp(lg * (c_len - 1.0 - idx)),
                       jnp.exp(lg * c_len)))

    for c in range(n_chunks):
        rows = slice(c * c_len, (c + 1) * c_len)
        for g in range(heads_per_step):
            decay_in, decay_q, decay_k, decay_chunk = decays[g]
            qk_cols = slice(g * dk, (g + 1) * dk)
            v_cols = slice(g * dv, (g + 1) * dv)
            q = q_ref[rows, qk_cols]
            k = k_ref[rows, qk_cols]
            v = v_ref[rows, v_cols]
            scores = lax.dot_general(q, k.astype(BF16), (((1,), (1,)), ((), ())),
                                     preferred_element_type=F32) * decay_in
            state = state_ref[g]
            y = _bdot(scores.astype(BF16), v) + _bdot(q, state.astype(BF16)) * decay_q
            kd_t = (k * decay_k).T.astype(BF16)
            state_ref[g] = state * decay_chunk + _bdot(kd_t, v)

            mu = jnp.mean(y, axis=-1, keepdims=True)
            yc = y - mu
            var = jnp.mean(yc * yc, axis=-1, keepdims=True)
            yn = yc * lax.rsqrt(var + NORM_EPS) * gn_ref[:, v_cols]
            o_ref[rows, v_cols] = (g_ref[rows, v_cols] * yn).astype(o_ref.dtype)


def _retention_core(q, k, v, gate, gn_g, log_gamma, *, batch, seq):
    t = q.shape[0]
    dk, dv = RET_QK_DIM, RET_V_DIM
    heads = v.shape[1] // dv
    hp = RET_HEADS_PER_STEP
    lb = RET_SEQ_BLOCK
    nb = seq // lb
    kern = functools.partial(_retention_kernel, heads_per_step=hp)
    return pl.pallas_call(
        kern,
        out_shape=jax.ShapeDtypeStruct((t, heads * dv), BF16),
        grid=(batch, heads // hp, nb),
        in_specs=[pl.BlockSpec(memory_space=pltpu.SMEM),
                  pl.BlockSpec((lb, hp * dk), lambda b, h, s: (b * nb + s, h)),
                  pl.BlockSpec((lb, hp * dk), lambda b, h, s: (b * nb + s, h)),
                  pl.BlockSpec((lb, hp * dv), lambda b, h, s: (b * nb + s, h)),
                  pl.BlockSpec((lb, hp * dv), lambda b, h, s: (b * nb + s, h)),
                  pl.BlockSpec((1, hp * dv), lambda b, h, s: (0, h))],
        out_specs=pl.BlockSpec((lb, hp * dv), lambda b, h, s: (b * nb + s, h)),
        scratch_shapes=[pltpu.VMEM((hp, dk, dv), F32)],
        compiler_params=_params("parallel", "parallel", "arbitrary"),
        name="retention_core",
    )(log_gamma, q, k, v, gate, gn_g)


def _mla_rope_table_kernel(pos_ref, inv_ref, cos_mask_ref, sin_sign_ref, tcos_ref, tsin_ref):
    ang = pos_ref[...].astype(F32) * inv_ref[...]
    tcos_ref[...] = jnp.cos(ang) * cos_mask_ref[...]
    tsin_ref[...] = jnp.sin(ang) * sin_sign_ref[...]


def _mla_rope_tables(pos_col):
    t = pos_col.shape[0]
    half = MLA_ROPE // 2
    inv = ROPE_THETA ** (-jnp.arange(0, MLA_ROPE, 2, dtype=F32) / MLA_ROPE)
    ones, zeros = jnp.ones((half,), F32), jnp.zeros((half,), F32)
    inv_row = jnp.concatenate([inv, inv, zeros, zeros]).reshape(1, LANES)
    cos_mask = jnp.concatenate([ones, ones, zeros, zeros]).reshape(1, LANES)
    sin_sign = jnp.concatenate([-ones, ones, zeros, zeros]).reshape(1, LANES)
    tm = ROW_TILE
    row = pl.BlockSpec((1, LANES), lambda i: (0, 0))
    return pl.pallas_call(
        _mla_rope_table_kernel,
        out_shape=(jax.ShapeDtypeStruct((t, LANES), F32), jax.ShapeDtypeStruct((t, LANES), F32)),
        grid=(t // tm,),
        in_specs=[pl.BlockSpec((tm, 1), lambda i: (i, 0)), row, row, row],
        out_specs=(pl.BlockSpec((tm, LANES), lambda i: (i, 0)),
                   pl.BlockSpec((tm, LANES), lambda i: (i, 0))),
        compiler_params=_params("parallel"),
        name="mla_rope_tables",
    )(pos_col, inv_row, cos_mask, sin_sign)


def _mla_in_kernel(h_ref, w_ref, gq_ref, gkv_ref, tcos_ref, tsin_ref,
                   cq_ref, ckv_ref, kr_ref, wb_ref, *, q_lora, kv_lora):
    @pl.when(pl.program_id(0) == 0)
    def _():
        wb_ref[...] = w_ref[...].astype(BF16)

    c = _bdot(h_ref[...], wb_ref[...])
    cq_ref[...] = _rms(c[:, :q_lora], gq_ref[...]).astype(cq_ref.dtype)
    ckv_ref[...] = _rms(c[:, q_lora:q_lora + kv_lora], gkv_ref[...]).astype(ckv_ref.dtype)
    kr = c[:, q_lora + kv_lora:]
    half = kr.shape[1] // 2
    v = jnp.concatenate([kr, jnp.zeros_like(kr)], axis=-1)
    lane = lax.broadcasted_iota(jnp.int32, v.shape, 1)
    swapped = jnp.where(lane < half, pltpu.roll(v, LANES - half, 1), pltpu.roll(v, half, 1))
    swapped = jnp.where(lane < 2 * half, swapped, 0.0)
    kr_ref[...] = v * tcos_ref[...] + swapped * tsin_ref[...]


def _mla_in(h, w, layer, gq, gkv, tcos, tsin):
    t, d = h.shape
    q_lora, kv_lora = gq.shape[1], gkv.shape[1]
    n_in = w.shape[2]
    tm = ROW_TILE // 2
    kern = functools.partial(_mla_in_kernel, q_lora=q_lora, kv_lora=kv_lora)
    return pl.pallas_call(
        kern,
        out_shape=(jax.ShapeDtypeStruct((t, q_lora), BF16),
                   jax.ShapeDtypeStruct((t, kv_lora), BF16),
                   jax.ShapeDtypeStruct((t, LANES), F32)),
        grid=(t // tm,),
        in_specs=[pl.BlockSpec((tm, d), lambda i: (i, 0)),
                  pl.BlockSpec((None, d, n_in), lambda i: (layer, 0, 0)),
                  pl.BlockSpec((1, q_lora), lambda i: (0, 0)),
                  pl.BlockSpec((1, kv_lora), lambda i: (0, 0)),
                  pl.BlockSpec((tm, LANES), lambda i: (i, 0)),
                  pl.BlockSpec((tm, LANES), lambda i: (i, 0))],
        out_specs=(pl.BlockSpec((tm, q_lora), lambda i: (i, 0)),
                   pl.BlockSpec((tm, kv_lora), lambda i: (i, 0)),
                   pl.BlockSpec((tm, LANES), lambda i: (i, 0))),
        scratch_shapes=[pltpu.VMEM((d, n_in), BF16)],
        compiler_params=_params("arbitrary"),
        name="mla_in",
    )(h, w, gq, gkv, tcos, tsin)


def _mla_q_weight_kernel(w_ref, o_ref):
    half = MLA_ROPE // 2
    w = w_ref[...]
    for t in range(o_ref.shape[0]):
        o = t * MLA_QD
        nope = w[:, o:o + MLA_NOPE]
        x1 = w[:, o + MLA_NOPE:o + MLA_NOPE + half]
        x2 = w[:, o + MLA_NOPE + half:o + MLA_QD]
        o_ref[t] = jnp.concatenate([nope, x1, x2, x2, x1], axis=-1).astype(o_ref.dtype)


def _mla_q_weights(w_uq, layer):
    q_lora = w_uq.shape[1]
    heads = w_uq.shape[2] // MLA_QD
    hp = 2
    return pl.pallas_call(
        _mla_q_weight_kernel,
        out_shape=jax.ShapeDtypeStruct((heads, q_lora, MLA_PAD), BF16),
        grid=(heads // hp,),
        in_specs=[pl.BlockSpec((None, q_lora, hp * MLA_QD), lambda h: (layer, 0, h))],
        out_specs=pl.BlockSpec((hp, q_lora, MLA_PAD), lambda h: (h, 0, 0)),
        compiler_params=_params("parallel"),
        name="mla_q_weights",
    )(w_uq)


def _mla_q_kernel(cq_ref, w_ref, tcos_ref, tsin_ref, o_ref, *, scale):
    cq = cq_ref[...]
    tcos, tsin = tcos_ref[...], tsin_ref[...]
    for g in range(o_ref.shape[0]):
        res = _bdot(cq, w_ref[g])
        rot = res[:, MLA_NOPE:]
        rope = rot * tcos + pltpu.roll(rot, LANES // 2, 1) * tsin
        o_ref[g, :, :MLA_NOPE] = (res[:, :MLA_NOPE] * scale).astype(o_ref.dtype)
        o_ref[g, :, MLA_NOPE:] = (rope * scale).astype(o_ref.dtype)


def _mla_q(cq, w_q, tcos, tsin):
    t, q_lora = cq.shape
    heads = w_q.shape[0]
    tm = ROW_TILE
    hp = MLA_HEADS_PER_STEP
    kern = functools.partial(_mla_q_kernel, scale=MLA_QD ** -0.5)
    return pl.pallas_call(
        kern,
        out_shape=jax.ShapeDtypeStruct((heads, t, MLA_PAD), BF16),
        grid=(t // tm, heads // hp),
        in_specs=[pl.BlockSpec((tm, q_lora), lambda i, h: (i, 0)),
                  pl.BlockSpec((hp, q_lora, MLA_PAD), lambda i, h: (h, 0, 0)),
                  pl.BlockSpec((tm, LANES), lambda i, h: (i, 0)),
                  pl.BlockSpec((tm, LANES), lambda i, h: (i, 0))],
        out_specs=pl.BlockSpec((hp, tm, MLA_PAD), lambda i, h: (h, i, 0)),
        compiler_params=_params("parallel", "parallel"),
        name="mla_q",
    )(cq, w_q, tcos, tsin)


def _mla_kv_kernel(ckv_ref, w_ref, kr_ref, k_ref, v_ref):
    ckv = ckv_ref[...]
    per_head = MLA_NOPE + MLA_V
    for g in range(k_ref.shape[0]):
        kv = _bdot(ckv, w_ref[:, g * per_head:(g + 1) * per_head].astype(BF16))
        k_ref[g, :, :MLA_NOPE] = kv[:, :MLA_NOPE].astype(k_ref.dtype)
        k_ref[g, :, MLA_NOPE:] = kr_ref[...].astype(k_ref.dtype)
        v_ref[g, :, :MLA_V] = kv[:, MLA_NOPE:].astype(v_ref.dtype)
        v_ref[g, :, MLA_V:] = jnp.ones((v_ref.shape[1], v_ref.shape[2] - MLA_V), v_ref.dtype)


def _mla_kv(ckv, w_ukv, layer, k_rope):
    t, kv_lora = ckv.shape
    per_head = MLA_NOPE + MLA_V
    heads = w_ukv.shape[2] // per_head
    tm = ROW_TILE
    hp = MLA_HEADS_PER_STEP
    return pl.pallas_call(
        _mla_kv_kernel,
        out_shape=(jax.ShapeDtypeStruct((heads, t, MLA_PAD), BF16),
                   jax.ShapeDtypeStruct((heads, t, 2 * MLA_V), BF16)),
        grid=(t // tm, heads // hp),
        in_specs=[pl.BlockSpec((tm, kv_lora), lambda i, h: (i, 0)),
                  pl.BlockSpec((None, kv_lora, hp * per_head), lambda i, h: (layer, 0, h)),
                  pl.BlockSpec((tm, LANES), lambda i, h: (i, 0))],
        out_specs=(pl.BlockSpec((hp, tm, MLA_PAD), lambda i, h: (h, i, 0)),
                   pl.BlockSpec((hp, tm, 2 * MLA_V), lambda i, h: (h, i, 0))),
        compiler_params=_params("parallel", "parallel"),
        name="mla_kv",
    )(ckv, w_ukv, k_rope)


def _flash_kernel(q_ref, k_ref, v_ref, o_ref, sa_ref, sb_ref, m_ref, acc_ref):
    n_heads, seq, _ = q_ref.shape
    tile = ATTN_TILE
    nq = seq // tile
    tiles = [(qi, ki) for qi in range(nq) for ki in range(qi + 1)]
    s_bufs = (sa_ref, sb_ref)

    def rows(i):
        return slice(i * tile, (i + 1) * tile)

    def scores_into(s_ref, qi, ki):
        for g in range(n_heads):
            s_ref[g] = lax.dot_general(q_ref[g, rows(qi), :], k_ref[g, rows(ki), :],
                                       (((1,), (1,)), ((), ())), preferred_element_type=F32)

    def consume(s_ref, qi, ki):
        slot = qi % 2
        for g in range(n_heads):
            s = s_ref[g]
            if ki == qi:
                row = lax.broadcasted_iota(jnp.int32, s.shape, 0)
                col = lax.broadcasted_iota(jnp.int32, s.shape, 1)
                s = jnp.where(col <= row, s, NEG_BIG)
            m_cur = jnp.max(s, axis=-1, keepdims=True)
            if ki == 0:
                m_new = m_cur
            else:
                m_prev = m_ref[slot, g]
                m_new = jnp.maximum(m_prev, m_cur)
                alpha = jnp.exp(m_prev - m_new)
            p = jnp.exp(s - m_new).astype(BF16)
            pv = _bdot(p, v_ref[g, rows(ki), :])
            acc = pv if ki == 0 else alpha * acc_ref[slot, g] + pv
            if ki == qi:
                o_ref[rows(qi), g * MLA_V:(g + 1) * MLA_V] = (acc[:, :MLA_V] / acc[:, MLA_V:]).astype(o_ref.dtype)
            else:
                acc_ref[slot, g] = acc
                m_ref[slot, g] = m_new

    def step(n):
        qi, ki = tiles[n]
        if n + 1 < len(tiles):
            scores_into(s_bufs[(n + 1) % 2], *tiles[n + 1])
        consume(s_bufs[n % 2], qi, ki)

    def steps(first):
        for n in range(first, min(first + ATTN_STEPS_PER_REGION, len(tiles))):
            step(n)

    one_trip = jnp.minimum(pl.program_id(0) + 1, 1)
    scores_into(s_bufs[0], *tiles[0])
    for first in range(0, len(tiles), ATTN_STEPS_PER_REGION):
        lax.fori_loop(0, one_trip, lambda _, carry, first=first: (steps(first), carry)[1], 0)


def _flash_attention(q, k, v, *, batch, seq):
    heads, t, _ = q.shape
    tile = ATTN_TILE
    hp = ATTN_HEADS_PER_STEP
    head_block = pl.BlockSpec((hp, seq, MLA_PAD), lambda b, h: (h, b, 0))
    return pl.pallas_call(
        _flash_kernel,
        out_shape=jax.ShapeDtypeStruct((t, heads * MLA_V), BF16),
        grid=(batch, heads // hp),
        in_specs=[head_block, head_block, head_block],
        out_specs=pl.BlockSpec((seq, hp * MLA_V), lambda b, h: (b, h)),
        scratch_shapes=[pltpu.VMEM((hp, tile, tile), F32),
                        pltpu.VMEM((hp, tile, tile), F32),
                        pltpu.VMEM((2, hp, tile, 1), F32),
                        pltpu.VMEM((2, hp, tile, 2 * MLA_V), F32)],
        compiler_params=_params("parallel", "parallel"),
        name="mla_flash_attention",
    )(q, k, v)


def _retention_layer(x, h, g_post, g_next, w_in, gn_g, w_out, layer, cos, sin, log_gamma, *, batch, seq):
    d = x.shape[1]
    dk, dv = RET_QK_DIM, RET_V_DIM
    heads = d // dk
    n_q = heads * dk
    n_v = heads * dv
    tn = PROJ_COL_TILE
    q = _proj_rope(h, w_in, layer, cos, sin, col0=0, n_cols=n_q, tn=tn, head_dim=dk,
                   scale=None, out_dtype=BF16, name="ret_q_proj")
    k = _proj_rope(h, w_in, layer, cos, sin, col0=n_q, n_cols=n_q, tn=tn, head_dim=dk,
                   scale=dk ** -0.5, out_dtype=F32, name="ret_k_proj")
    v = _proj(h, w_in, layer, col0=2 * n_q, n_cols=n_v, tn=tn, out_dtype=BF16, name="ret_v_proj")
    gate = _proj(h, w_in, layer, col0=2 * n_q + n_v, n_cols=n_v, tn=tn, out_dtype=BF16,
                 name="ret_gate_proj", activation=_silu)
    y = _retention_core(q, k, v, gate, gn_g, log_gamma, batch=batch, seq=seq)
    return _out_proj(y, w_out, layer, x, g_post, g_next, tk=OUT_K_TILE, name="ret_out_proj")


def _mla_layer(x, h, g_post, g_next, w_in, g_q, g_kv, w_uq, w_ukv, w_out, layer, tcos, tsin, *, batch, seq):
    cq, ckv, k_rope = _mla_in(h, w_in, layer, g_q, g_kv, tcos, tsin)
    q = _mla_q(cq, _mla_q_weights(w_uq, layer), tcos, tsin)
    k, v = _mla_kv(ckv, w_ukv, layer, k_rope)
    o = _flash_attention(q, k, v, batch=batch, seq=seq)
    return _out_proj(o, w_out, layer, x, g_post, g_next, tk=OUT_K_TILE, name="mla_out_proj")


def _ffn(x, h, g_post, g_next, w_gu, w_down, layer):
    act = _proj_swiglu(h, w_gu, layer, tn=FFN_COL_TILE, name="ffn_gate_up")
    return _out_proj(act, w_down, layer, x, g_post, g_next, tk=OUT_K_TILE, name="ffn_down_proj")


def kernel(x, positions, norm_mix_pre, norm_mix_post, norm_ffn_pre, norm_ffn_post, ret_w_in, ret_gn_g, ret_w_out, mla_w_in, mla_g_q, mla_g_kv, mla_w_uq, mla_w_ukv, mla_w_out, ffn_w_gu, ffn_w_down):
    batch, seq, d = x.shape
    depth = norm_mix_pre.shape[0]
    t = batch * seq
    xf = x.reshape(t, d)
    pos_col = positions.reshape(t, 1)
    cos_r, sin_r = _rope_tables(pos_col, RET_QK_DIM)
    tcos_m, tsin_m = _mla_rope_tables(pos_col)
    ret_heads = d // RET_QK_DIM
    log_gamma = jnp.log1p(-jnp.exp2(-5.0 - jnp.arange(ret_heads, dtype=F32)))

    def row(a, i):
        return a[i].reshape(1, -1)

    h = _prenorm(xf, row(norm_mix_pre, 0))
    for i in range(depth):
        j = i // N_MIXERS
        g_ffn_pre = row(norm_ffn_pre, i)
        g_next_mix = row(norm_mix_pre, i + 1) if i + 1 < depth else None
        if i % N_MIXERS == 0:
            xf, h = _retention_layer(xf, h, row(norm_mix_post, i), g_ffn_pre, ret_w_in, row(ret_gn_g, j),
                                     ret_w_out, j, cos_r, sin_r, log_gamma, batch=batch, seq=seq)
        else:
            xf, h = _mla_layer(xf, h, row(norm_mix_post, i), g_ffn_pre, mla_w_in, row(mla_g_q, j),
                               row(mla_g_kv, j), mla_w_uq, mla_w_ukv, mla_w_out, j, tcos_m, tsin_m,
                               batch=batch, seq=seq)
        xf, h = _ffn(xf, h, row(norm_ffn_post, i), g_next_mix, ffn_w_gu, ffn_w_down, i)
    return xf.reshape(batch, seq, d)
```

```python
import functools

import jax
import jax.numpy as jnp
from jax import lax
from jax.experimental import pallas as pl
from jax.experimental.pallas import tpu as pltpu

F32 = jnp.float32
BF16 = jnp.bfloat16

RET_QK_DIM = 256
RET_V_DIM = 2 * RET_QK_DIM
MLA_NOPE = 128
MLA_ROPE = 64
MLA_V = 128
MLA_QD = MLA_NOPE + MLA_ROPE
MLA_PAD = 256
LANES = 128
ROPE_THETA = 10000.0
NORM_EPS = 1e-6
N_MIXERS = 2

V7X_VMEM_BYTES = 64 * 1024 * 1024
VMEM_LIMIT_BYTES = V7X_VMEM_BYTES - 8 * 1024 * 1024

ROW_TILE = 1024
PROJ_COL_TILE = 1024
FFN_COL_TILE = 512
OUT_ROW_TILE = 512
FFN_DOWN_K_PARTS = 2
RET_CHUNK = 256
RET_SEQ_BLOCK = 1024
RET_HEADS_PER_STEP = 2
MLA_HEADS_PER_STEP = 4
ATTN_TILE = 512
ATTN_HEADS_PER_STEP = 2
ATTN_STEPS_PER_REGION = 2
OUT_COL_CHUNK = 512
NEG_BIG = -1e30


def _params(*sem):
    return pltpu.CompilerParams(dimension_semantics=sem, vmem_limit_bytes=VMEM_LIMIT_BYTES)


def _rms(x, g):
    return x * lax.rsqrt(jnp.mean(x * x, axis=-1, keepdims=True) + NORM_EPS) * g


def _bdot(a, b):
    return jnp.dot(a, b, preferred_element_type=F32)


def _silu(x):
    return x * jax.nn.sigmoid(x)


def _rope_table_kernel(pos_ref, inv_ref, cos_ref, sin_ref):
    ang = pos_ref[...].astype(F32) * inv_ref[...]
    cos_ref[...] = jnp.cos(ang)
    sin_ref[...] = jnp.sin(ang)


def _rope_tables(pos_col, d):
    t = pos_col.shape[0]
    f = d // 2
    inv = (ROPE_THETA ** (-jnp.arange(0, d, 2, dtype=F32) / d)).reshape(1, f)
    tm = ROW_TILE
    return pl.pallas_call(
        _rope_table_kernel,
        out_shape=(jax.ShapeDtypeStruct((t, f), F32), jax.ShapeDtypeStruct((t, f), F32)),
        grid=(t // tm,),
        in_specs=[pl.BlockSpec((tm, 1), lambda i: (i, 0)),
                  pl.BlockSpec((1, f), lambda i: (0, 0))],
        out_specs=(pl.BlockSpec((tm, f), lambda i: (i, 0)),
                   pl.BlockSpec((tm, f), lambda i: (i, 0))),
        compiler_params=_params("parallel"),
        name=f"rope_tables_{d}",
    )(pos_col, inv)


def _prenorm_kernel(x_ref, g_ref, h_ref):
    h_ref[...] = _rms(x_ref[...], g_ref[...]).astype(h_ref.dtype)


def _prenorm(x, g):
    t, d = x.shape
    tm = ROW_TILE // 2
    return pl.pallas_call(
        _prenorm_kernel,
        out_shape=jax.ShapeDtypeStruct((t, d), BF16),
        grid=(t // tm,),
        in_specs=[pl.BlockSpec((tm, d), lambda i: (i, 0)),
                  pl.BlockSpec((1, d), lambda i: (0, 0))],
        out_specs=pl.BlockSpec((tm, d), lambda i: (i, 0)),
        compiler_params=_params("parallel"),
        name="prenorm",
    )(x, g)


def _cast_weight_once(w_ref, wb_ref):
    @pl.when(pl.program_id(1) == 0)
    def _():
        wb_ref[...] = w_ref[...].astype(BF16)


def _proj_plain_kernel(h_ref, w_ref, *rest, activation, with_side_cast):
    if with_side_cast:
        side_ref, o_ref, side_o_ref, wb_ref = rest
        side_o_ref[...] = side_ref[...].astype(side_o_ref.dtype)
    else:
        o_ref, wb_ref = rest
    _cast_weight_once(w_ref, wb_ref)
    acc = _bdot(h_ref[...], wb_ref[...])
    if activation is not None:
        acc = activation(acc)
    o_ref[...] = acc.astype(o_ref.dtype)


def _proj_rope_kernel(h_ref, w_ref, cos_ref, sin_ref, o_ref, wb_ref, *, head_dim, scale):
    _cast_weight_once(w_ref, wb_ref)
    acc = _bdot(h_ref[...], wb_ref[...])
    cos, sin = cos_ref[...], sin_ref[...]
    half = head_dim // 2
    for c0 in range(0, acc.shape[1], head_dim):
        x1, x2 = acc[:, c0:c0 + half], acc[:, c0 + half:c0 + head_dim]
        r1 = x1 * cos - x2 * sin
        r2 = x1 * sin + x2 * cos
        if scale is not None:
            r1, r2 = r1 * scale, r2 * scale
        o_ref[:, c0:c0 + half] = r1.astype(o_ref.dtype)
        o_ref[:, c0 + half:c0 + head_dim] = r2.astype(o_ref.dtype)


def _proj_swiglu_kernel(h_ref, wg_ref, wu_ref, side_ref, o_ref, side_o_ref, wgb_ref, wub_ref):
    side_o_ref[...] = side_ref[...].astype(side_o_ref.dtype)
    _cast_weight_once(wg_ref, wgb_ref)
    _cast_weight_once(wu_ref, wub_ref)
    h = h_ref[...]
    gate = _bdot(h, wgb_ref[...])
    up = _bdot(h, wub_ref[...])
    o_ref[...] = (_silu(gate) * up).astype(o_ref.dtype)


def _side_cast_specs(side_w, layer, n_steps, n_inner):
    _, kdim, n = side_w.shape
    rows = kdim // n_steps
    assert rows * n_steps == kdim and rows % 16 == 0, (kdim, n_steps)
    in_spec = pl.BlockSpec((None, rows, n), lambda j, i: (layer, j * n_inner + i, 0))
    out_spec = pl.BlockSpec((rows, n), lambda j, i: (j * n_inner + i, 0))
    return in_spec, out_spec, jax.ShapeDtypeStruct((kdim, n), BF16)


def _proj(h, w, layer, *, col0, n_cols, tn, out_dtype, name, activation=None, side_w=None):
    t, kdim = h.shape
    tm = ROW_TILE
    tn = min(tn, n_cols)
    off = col0 // tn
    grid = (n_cols // tn, t // tm)
    operands = [h, w]
    in_specs = [pl.BlockSpec((tm, kdim), lambda j, i: (i, 0)),
                pl.BlockSpec((None, kdim, tn), lambda j, i: (layer, 0, j + off))]
    out_shape = jax.ShapeDtypeStruct((t, n_cols), out_dtype)
    out_specs = pl.BlockSpec((tm, tn), lambda j, i: (i, j))
    if side_w is not None:
        side_in, side_out, side_shape = _side_cast_specs(side_w, layer, grid[0] * grid[1], grid[1])
        operands.append(side_w)
        in_specs.append(side_in)
        out_shape, out_specs = (out_shape, side_shape), (out_specs, side_out)
    return pl.pallas_call(
        functools.partial(_proj_plain_kernel, activation=activation, with_side_cast=side_w is not None),
        out_shape=out_shape,
        grid=grid,
        in_specs=in_specs,
        out_specs=out_specs,
        scratch_shapes=[pltpu.VMEM((kdim, tn), BF16)],
        compiler_params=_params("arbitrary", "arbitrary"),
        name=name,
    )(*operands)


def _proj_rope(h, w, layer, cos, sin, *, col0, n_cols, tn, head_dim, scale, out_dtype, name):
    t, kdim = h.shape
    tm = ROW_TILE
    tn = min(tn, n_cols)
    off = col0 // tn
    kern = functools.partial(_proj_rope_kernel, head_dim=head_dim, scale=scale)
    return pl.pallas_call(
        kern,
        out_shape=jax.ShapeDtypeStruct((t, n_cols), out_dtype),
        grid=(n_cols // tn, t // tm),
        in_specs=[pl.BlockSpec((tm, kdim), lambda j, i: (i, 0)),
                  pl.BlockSpec((None, kdim, tn), lambda j, i: (layer, 0, j + off)),
                  pl.BlockSpec((tm, head_dim // 2), lambda j, i: (i, 0)),
                  pl.BlockSpec((tm, head_dim // 2), lambda j, i: (i, 0))],
        out_specs=pl.BlockSpec((tm, tn), lambda j, i: (i, j)),
        scratch_shapes=[pltpu.VMEM((kdim, tn), BF16)],
        compiler_params=_params("arbitrary", "arbitrary"),
        name=name,
    )(h, w, cos, sin)


def _proj_swiglu(h, w_gu, layer, side_w, *, tn, name):
    t, kdim = h.shape
    d_ff = w_gu.shape[2] // 2
    tm = ROW_TILE
    n_tiles = d_ff // tn
    grid = (n_tiles, t // tm)
    side_in, side_out, side_shape = _side_cast_specs(side_w, layer, grid[0] * grid[1], grid[1])
    return pl.pallas_call(
        _proj_swiglu_kernel,
        out_shape=(jax.ShapeDtypeStruct((t, d_ff), BF16), side_shape),
        grid=grid,
        in_specs=[pl.BlockSpec((tm, kdim), lambda j, i: (i, 0)),
                  pl.BlockSpec((None, kdim, tn), lambda j, i: (layer, 0, j)),
                  pl.BlockSpec((None, kdim, tn), lambda j, i: (layer, 0, j + n_tiles)),
                  side_in],
        out_specs=(pl.BlockSpec((tm, tn), lambda j, i: (i, j)), side_out),
        scratch_shapes=[pltpu.VMEM((kdim, tn), BF16), pltpu.VMEM((kdim, tn), BF16)],
        compiler_params=_params("arbitrary", "arbitrary"),
        name=name,
    )(h, w_gu, w_gu, side_w)


def _out_proj_kernel(a_ref, w_ref, x_ref, g_ref, *rest, n_k, emit_next):
    if emit_next:
        gn_ref, o_ref, h_ref = rest
    else:
        (o_ref,) = rest
    k = pl.program_id(1)
    tk = a_ref.shape[1]

    def accumulate(kk):
        a = a_ref[...]
        for n in range(0, o_ref.shape[1], OUT_COL_CHUNK):
            cols = slice(n, n + OUT_COL_CHUNK)
            part = _bdot(a, w_ref[kk * tk:(kk + 1) * tk, cols])
            if kk == 0:
                o_ref[:, cols] = part
            else:
                o_ref[:, cols] += part

    def epilogue():
        x_new = x_ref[...] + _rms(o_ref[...], g_ref[...])
        o_ref[...] = x_new
        if emit_next:
            h_ref[...] = _rms(x_new, gn_ref[...]).astype(h_ref.dtype)

    for kk in range(n_k):
        @pl.when(k == kk)
        def _(kk=kk):
            accumulate(kk)
            if kk == n_k - 1:
                epilogue()


def _out_proj(a, w_bf16, x, g_post, g_next, *, n_k, name):
    t, kdim = a.shape
    d = w_bf16.shape[1]
    tm = OUT_ROW_TILE
    tk = kdim // n_k
    emit_next = g_next is not None
    kern = functools.partial(_out_proj_kernel, n_k=n_k, emit_next=emit_next)
    row_spec = pl.BlockSpec((tm, d), lambda i, k: (i, 0))
    gain_spec = pl.BlockSpec((1, d), lambda i, k: (0, 0))
    operands = [a, w_bf16, x, g_post]
    in_specs = [pl.BlockSpec((tm, tk), lambda i, k: (i, k)),
                pl.BlockSpec((kdim, d), lambda i, k: (0, 0), pipeline_mode=pl.Buffered(1)),
                row_spec, gain_spec]
    out_shape = [jax.ShapeDtypeStruct((t, d), F32)]
    out_specs = [row_spec]
    if emit_next:
        operands.append(g_next)
        in_specs.append(gain_spec)
        out_shape.append(jax.ShapeDtypeStruct((t, d), BF16))
        out_specs.append(row_spec)
    res = pl.pallas_call(
        kern,
        out_shape=tuple(out_shape),
        grid=(t // tm, n_k),
        in_specs=in_specs,
        out_specs=tuple(out_specs),
        compiler_params=_params("arbitrary", "arbitrary"),
        name=name,
    )(*operands)
    return (res[0], res[1]) if emit_next else (res[0], None)


def _retention_kernel(lg_ref, q_ref, k_ref, v_ref, g_ref, gn_ref, o_ref, state_ref, *, heads_per_step):
    c_len = RET_CHUNK
    dk, dv = RET_QK_DIM, RET_V_DIM
    n_chunks = q_ref.shape[0] // c_len

    @pl.when(pl.program_id(2) == 0)
    def _():
        state_ref[...] = jnp.zeros_like(state_ref)

    row = lax.broadcasted_iota(jnp.int32, (c_len, c_len), 0)
    col = lax.broadcasted_iota(jnp.int32, (c_len, c_len), 1)
    rel = (row - col).astype(F32)
    causal = rel >= 0
    rel_pos = jnp.where(causal, rel, 0.0)
    idx = lax.broadcasted_iota(jnp.int32, (c_len, 1), 0).astype(F32)
    decays = []
    for g in range(heads_per_step):
        lg = jnp.full((1, 1), lg_ref[pl.program_id(1) * heads_per_step + g], F32)
        decays.append((jnp.where(causal, jnp.exp(lg * rel_pos), 0.0),
                       jnp.exp(lg * (idx + 1.0)),
                       jnp.exp(lg * (c_len - 1.0 - idx)),
                       jnp.exp(lg * c_len)))

    for c in range(n_chunks):
        rows = slice(c * c_len, (c + 1) * c_len)
        for g in range(heads_per_step):
            decay_in, decay_q, decay_k, decay_chunk = decays[g]
            qk_cols = slice(g * dk, (g + 1) * dk)
            v_cols = slice(g * dv, (g + 1) * dv)
            q = q_ref[rows, qk_cols]
            k = k_ref[rows, qk_cols]
            v = v_ref[rows, v_cols]
            scores = lax.dot_general(q, k.astype(BF16), (((1,), (1,)), ((), ())),
                                     preferred_element_type=F32) * decay_in
            state = state_ref[g]
            y = _bdot(scores.astype(BF16), v) + _bdot(q, state.astype(BF16)) * decay_q
            kd_t = (k * decay_k).T.astype(BF16)
            state_ref[g] = state * decay_chunk + _bdot(kd_t, v)

            mu = jnp.mean(y, axis=-1, keepdims=True)
            yc = y - mu
            var = jnp.mean(yc * yc, axis=-1, keepdims=True)
            yn = yc * lax.rsqrt(var + NORM_EPS) * gn_ref[:, v_cols]
            o_ref[rows, v_cols] = (g_ref[rows, v_cols] * yn).astype(o_ref.dtype)


def _retention_core(q, k, v, gate, gn_g, log_gamma, *, batch, seq):
    t = q.shape[0]
    dk, dv = RET_QK_DIM, RET_V_DIM
    heads = v.shape[1] // dv
    hp = RET_HEADS_PER_STEP
    lb = RET_SEQ_BLOCK
    nb = seq // lb
    kern = functools.partial(_retention_kernel, heads_per_step=hp)
    return pl.pallas_call(
        kern,
        out_shape=jax.ShapeDtypeStruct((t, heads * dv), BF16),
        grid=(batch, heads // hp, nb),
        in_specs=[pl.BlockSpec(memory_space=pltpu.SMEM),
                  pl.BlockSpec((lb, hp * dk), lambda b, h, s: (b * nb + s, h)),
                  pl.BlockSpec((lb, hp * dk), lambda b, h, s: (b * nb + s, h)),
                  pl.BlockSpec((lb, hp * dv), lambda b, h, s: (b * nb + s, h)),
                  pl.BlockSpec((lb, hp * dv), lambda b, h, s: (b * nb + s, h)),
                  pl.BlockSpec((1, hp * dv), lambda b, h, s: (0, h))],
        out_specs=pl.BlockSpec((lb, hp * dv), lambda b, h, s: (b * nb + s, h)),
        scratch_shapes=[pltpu.VMEM((hp, dk, dv), F32)],
        compiler_params=_params("parallel", "parallel", "arbitrary"),
        name="retention_core",
    )(log_gamma, q, k, v, gate, gn_g)


def _mla_rope_table_kernel(pos_ref, inv_ref, cos_mask_ref, sin_sign_ref, tcos_ref, tsin_ref):
    ang = pos_ref[...].astype(F32) * inv_ref[...]
    tcos_ref[...] = jnp.cos(ang) * cos_mask_ref[...]
    tsin_ref[...] = jnp.sin(ang) * sin_sign_ref[...]


def _mla_rope_tables(pos_col):
    t = pos_col.shape[0]
    half = MLA_ROPE // 2
    inv = ROPE_THETA ** (-jnp.arange(0, MLA_ROPE, 2, dtype=F32) / MLA_ROPE)
    ones, zeros = jnp.ones((half,), F32), jnp.zeros((half,), F32)
    inv_row = jnp.concatenate([inv, inv, zeros, zeros]).reshape(1, LANES)
    cos_mask = jnp.concatenate([ones, ones, zeros, zeros]).reshape(1, LANES)
    sin_sign = jnp.concatenate([-ones, ones, zeros, zeros]).reshape(1, LANES)
    tm = ROW_TILE
    row = pl.BlockSpec((1, LANES), lambda i: (0, 0))
    return pl.pallas_call(
        _mla_rope_table_kernel,
        out_shape=(jax.ShapeDtypeStruct((t, LANES), F32), jax.ShapeDtypeStruct((t, LANES), F32)),
        grid=(t // tm,),
        in_specs=[pl.BlockSpec((tm, 1), lambda i: (i, 0)), row, row, row],
        out_specs=(pl.BlockSpec((tm, LANES), lambda i: (i, 0)),
                   pl.BlockSpec((tm, LANES), lambda i: (i, 0))),
        compiler_params=_params("parallel"),
        name="mla_rope_tables",
    )(pos_col, inv_row, cos_mask, sin_sign)


def _mla_in_kernel(h_ref, w_ref, gq_ref, gkv_ref, tcos_ref, tsin_ref,
                   cq_ref, ckv_ref, kr_ref, wb_ref, *, q_lora, kv_lora):
    @pl.when(pl.program_id(0) == 0)
    def _():
        wb_ref[...] = w_ref[...].astype(BF16)

    c = _bdot(h_ref[...], wb_ref[...])
    cq_ref[...] = _rms(c[:, :q_lora], gq_ref[...]).astype(cq_ref.dtype)
    ckv_ref[...] = _rms(c[:, q_lora:q_lora + kv_lora], gkv_ref[...]).astype(ckv_ref.dtype)
    kr = c[:, q_lora + kv_lora:]
    half = kr.shape[1] // 2
    v = jnp.concatenate([kr, jnp.zeros_like(kr)], axis=-1)
    lane = lax.broadcasted_iota(jnp.int32, v.shape, 1)
    swapped = jnp.where(lane < half, pltpu.roll(v, LANES - half, 1), pltpu.roll(v, half, 1))
    swapped = jnp.where(lane < 2 * half, swapped, 0.0)
    kr_ref[...] = v * tcos_ref[...] + swapped * tsin_ref[...]


def _mla_in(h, w, layer, gq, gkv, tcos, tsin):
    t, d = h.shape
    q_lora, kv_lora = gq.shape[1], gkv.shape[1]
    n_in = w.shape[2]
    tm = ROW_TILE // 2
    kern = functools.partial(_mla_in_kernel, q_lora=q_lora, kv_lora=kv_lora)
    return pl.pallas_call(
        kern,
        out_shape=(jax.ShapeDtypeStruct((t, q_lora), BF16),
                   jax.ShapeDtypeStruct((t, kv_lora), BF16),
                   jax.ShapeDtypeStruct((t, LANES), F32)),
        grid=(t // tm,),
        in_specs=[pl.BlockSpec((tm, d), lambda i: (i, 0)),
                  pl.BlockSpec((None, d, n_in), lambda i: (layer, 0, 0)),
                  pl.BlockSpec((1, q_lora), lambda i: (0, 0)),
                  pl.BlockSpec((1, kv_lora), lambda i: (0, 0)),
                  pl.BlockSpec((tm, LANES), lambda i: (i, 0)),
                  pl.BlockSpec((tm, LANES), lambda i: (i, 0))],
        out_specs=(pl.BlockSpec((tm, q_lora), lambda i: (i, 0)),
                   pl.BlockSpec((tm, kv_lora), lambda i: (i, 0)),
                   pl.BlockSpec((tm, LANES), lambda i: (i, 0))),
        scratch_shapes=[pltpu.VMEM((d, n_in), BF16)],
        compiler_params=_params("arbitrary"),
        name="mla_in",
    )(h, w, gq, gkv, tcos, tsin)


def _mla_q_weight_kernel(w_ref, o_ref):
    half = MLA_ROPE // 2
    w = w_ref[...]
    for t in range(o_ref.shape[0]):
        o = t * MLA_QD
        nope = w[:, o:o + MLA_NOPE]
        x1 = w[:, o + MLA_NOPE:o + MLA_NOPE + half]
        x2 = w[:, o + MLA_NOPE + half:o + MLA_QD]
        o_ref[t] = jnp.concatenate([nope, x1, x2, x2, x1], axis=-1).astype(o_ref.dtype)


def _mla_q_weights(w_uq, layer):
    q_lora = w_uq.shape[1]
    heads = w_uq.shape[2] // MLA_QD
    hp = 2
    return pl.pallas_call(
        _mla_q_weight_kernel,
        out_shape=jax.ShapeDtypeStruct((heads, q_lora, MLA_PAD), BF16),
        grid=(heads // hp,),
        in_specs=[pl.BlockSpec((None, q_lora, hp * MLA_QD), lambda h: (layer, 0, h))],
        out_specs=pl.BlockSpec((hp, q_lora, MLA_PAD), lambda h: (h, 0, 0)),
        compiler_params=_params("parallel"),
        name="mla_q_weights",
    )(w_uq)


def _mla_q_kernel(cq_ref, w_ref, tcos_ref, tsin_ref, o_ref, *, scale):
    cq = cq_ref[...]
    tcos, tsin = tcos_ref[...], tsin_ref[...]
    for g in range(o_ref.shape[0]):
        res = _bdot(cq, w_ref[g])
        rot = res[:, MLA_NOPE:]
        rope = rot * tcos + pltpu.roll(rot, LANES // 2, 1) * tsin
        o_ref[g, :, :MLA_NOPE] = (res[:, :MLA_NOPE] * scale).astype(o_ref.dtype)
        o_ref[g, :, MLA_NOPE:] = (rope * scale).astype(o_ref.dtype)


def _mla_q(cq, w_q, tcos, tsin):
    t, q_lora = cq.shape
    heads = w_q.shape[0]
    tm = ROW_TILE
    hp = MLA_HEADS_PER_STEP
    kern = functools.partial(_mla_q_kernel, scale=MLA_QD ** -0.5)
    return pl.pallas_call(
        kern,
        out_shape=jax.ShapeDtypeStruct((heads, t, MLA_PAD), BF16),
        grid=(t // tm, heads // hp),
        in_specs=[pl.BlockSpec((tm, q_lora), lambda i, h: (i, 0)),
                  pl.BlockSpec((hp, q_lora, MLA_PAD), lambda i, h: (h, 0, 0)),
                  pl.BlockSpec((tm, LANES), lambda i, h: (i, 0)),
                  pl.BlockSpec((tm, LANES), lambda i, h: (i, 0))],
        out_specs=pl.BlockSpec((hp, tm, MLA_PAD), lambda i, h: (h, i, 0)),
        compiler_params=_params("parallel", "parallel"),
        name="mla_q",
    )(cq, w_q, tcos, tsin)


def _mla_kv_kernel(ckv_ref, w_ref, kr_ref, side_ref, k_ref, v_ref, side_o_ref):
    side_o_ref[...] = side_ref[...].astype(side_o_ref.dtype)
    ckv = ckv_ref[...]
    per_head = MLA_NOPE + MLA_V
    for g in range(k_ref.shape[0]):
        kv = _bdot(ckv, w_ref[:, g * per_head:(g + 1) * per_head].astype(BF16))
        k_ref[g, :, :MLA_NOPE] = kv[:, :MLA_NOPE].astype(k_ref.dtype)
        k_ref[g, :, MLA_NOPE:] = kr_ref[...].astype(k_ref.dtype)
        v_ref[g, :, :MLA_V] = kv[:, MLA_NOPE:].astype(v_ref.dtype)
        v_ref[g, :, MLA_V:] = jnp.ones((v_ref.shape[1], v_ref.shape[2] - MLA_V), v_ref.dtype)


def _mla_kv(ckv, w_ukv, layer, k_rope, side_w):
    t, kv_lora = ckv.shape
    per_head = MLA_NOPE + MLA_V
    heads = w_ukv.shape[2] // per_head
    tm = ROW_TILE
    hp = MLA_HEADS_PER_STEP
    grid = (t // tm, heads // hp)
    side_in, side_out, side_shape = _side_cast_specs(side_w, layer, grid[0] * grid[1], grid[1])
    return pl.pallas_call(
        _mla_kv_kernel,
        out_shape=(jax.ShapeDtypeStruct((heads, t, MLA_PAD), BF16),
                   jax.ShapeDtypeStruct((heads, t, 2 * MLA_V), BF16),
                   side_shape),
        grid=grid,
        in_specs=[pl.BlockSpec((tm, kv_lora), lambda i, h: (i, 0)),
                  pl.BlockSpec((None, kv_lora, hp * per_head), lambda i, h: (layer, 0, h)),
                  pl.BlockSpec((tm, LANES), lambda i, h: (i, 0)),
                  side_in],
        out_specs=(pl.BlockSpec((hp, tm, MLA_PAD), lambda i, h: (h, i, 0)),
                   pl.BlockSpec((hp, tm, 2 * MLA_V), lambda i, h: (h, i, 0)),
                   side_out),
        compiler_params=_params("arbitrary", "arbitrary"),
        name="mla_kv",
    )(ckv, w_ukv, k_rope, side_w)


def _flash_kernel(q_ref, k_ref, v_ref, o_ref, sa_ref, sb_ref, m_ref, acc_ref):
    n_heads, seq, _ = q_ref.shape
    tile = ATTN_TILE
    nq = seq // tile
    tiles = [(qi, ki) for qi in range(nq) for ki in range(qi + 1)]
    s_bufs = (sa_ref, sb_ref)

    def rows(i):
        return slice(i * tile, (i + 1) * tile)

    def scores_into(s_ref, qi, ki):
        for g in range(n_heads):
            s_ref[g] = lax.dot_general(q_ref[g, rows(qi), :], k_ref[g, rows(ki), :],
                                       (((1,), (1,)), ((), ())), preferred_element_type=F32)

    def consume(s_ref, qi, ki):
        slot = qi % 2
        for g in range(n_heads):
            s = s_ref[g]
            if ki == qi:
                row = lax.broadcasted_iota(jnp.int32, s.shape, 0)
                col = lax.broadcasted_iota(jnp.int32, s.shape, 1)
                s = jnp.where(col <= row, s, NEG_BIG)
            m_cur = jnp.max(s, axis=-1, keepdims=True)
            if ki == 0:
                m_new = m_cur
            else:
                m_prev = m_ref[slot, g]
                m_new = jnp.maximum(m_prev, m_cur)
                alpha = jnp.exp(m_prev - m_new)
            p = jnp.exp(s - m_new).astype(BF16)
            pv = _bdot(p, v_ref[g, rows(ki), :])
            acc = pv if ki == 0 else alpha * acc_ref[slot, g] + pv
            if ki == qi:
                o_ref[rows(qi), g * MLA_V:(g + 1) * MLA_V] = (acc[:, :MLA_V] / acc[:, MLA_V:]).astype(o_ref.dtype)
            else:
                acc_ref[slot, g] = acc
                m_ref[slot, g] = m_new

    def step(n):
        qi, ki = tiles[n]
        if n + 1 < len(tiles):
            scores_into(s_bufs[(n + 1) % 2], *tiles[n + 1])
        consume(s_bufs[n % 2], qi, ki)

    def steps(first):
        for n in range(first, min(first + ATTN_STEPS_PER_REGION, len(tiles))):
            step(n)

    one_trip = jnp.minimum(pl.program_id(0) + 1, 1)
    scores_into(s_bufs[0], *tiles[0])
    for first in range(0, len(tiles), ATTN_STEPS_PER_REGION):
        lax.fori_loop(0, one_trip, lambda _, carry, first=first: (steps(first), carry)[1], 0)


def _flash_attention(q, k, v, *, batch, seq):
    heads, t, _ = q.shape
    tile = ATTN_TILE
    hp = ATTN_HEADS_PER_STEP
    head_block = pl.BlockSpec((hp, seq, MLA_PAD), lambda b, h: (h, b, 0))
    return pl.pallas_call(
        _flash_kernel,
        out_shape=jax.ShapeDtypeStruct((t, heads * MLA_V), BF16),
        grid=(batch, heads // hp),
        in_specs=[head_block, head_block, head_block],
        out_specs=pl.BlockSpec((seq, hp * MLA_V), lambda b, h: (b, h)),
        scratch_shapes=[pltpu.VMEM((hp, tile, tile), F32),
                        pltpu.VMEM((hp, tile, tile), F32),
                        pltpu.VMEM((2, hp, tile, 1), F32),
                        pltpu.VMEM((2, hp, tile, 2 * MLA_V), F32)],
        compiler_params=_params("parallel", "parallel"),
        name="mla_flash_attention",
    )(q, k, v)


def _retention_layer(x, h, g_post, g_next, w_in, gn_g, w_out, layer, cos, sin, log_gamma, *, batch, seq):
    d = x.shape[1]
    dk, dv = RET_QK_DIM, RET_V_DIM
    heads = d // dk
    n_q = heads * dk
    n_v = heads * dv
    tn = PROJ_COL_TILE
    q = _proj_rope(h, w_in, layer, cos, sin, col0=0, n_cols=n_q, tn=tn, head_dim=dk,
                   scale=None, out_dtype=BF16, name="ret_q_proj")
    k = _proj_rope(h, w_in, layer, cos, sin, col0=n_q, n_cols=n_q, tn=tn, head_dim=dk,
                   scale=dk ** -0.5, out_dtype=F32, name="ret_k_proj")
    v, w_out_b = _proj(h, w_in, layer, col0=2 * n_q, n_cols=n_v, tn=tn, out_dtype=BF16,
                       name="ret_v_proj", side_w=w_out)
    gate = _proj(h, w_in, layer, col0=2 * n_q + n_v, n_cols=n_v, tn=tn, out_dtype=BF16,
                 name="ret_gate_proj", activation=_silu)
    y = _retention_core(q, k, v, gate, gn_g, log_gamma, batch=batch, seq=seq)
    return _out_proj(y, w_out_b, x, g_post, g_next, n_k=1, name="ret_out_proj")


def _mla_layer(x, h, g_post, g_next, w_in, g_q, g_kv, w_uq, w_ukv, w_out, layer, tcos, tsin, *, batch, seq):
    cq, ckv, k_rope = _mla_in(h, w_in, layer, g_q, g_kv, tcos, tsin)
    q = _mla_q(cq, _mla_q_weights(w_uq, layer), tcos, tsin)
    k, v, w_out_b = _mla_kv(ckv, w_ukv, layer, k_rope, w_out)
    o = _flash_attention(q, k, v, batch=batch, seq=seq)
    return _out_proj(o, w_out_b, x, g_post, g_next, n_k=1, name="mla_out_proj")


def _ffn(x, h, g_post, g_next, w_gu, w_down, layer):
    act, w_down_b = _proj_swiglu(h, w_gu, layer, w_down, tn=FFN_COL_TILE, name="ffn_gate_up")
    return _out_proj(act, w_down_b, x, g_post, g_next, n_k=FFN_DOWN_K_PARTS, name="ffn_down_proj")


def kernel(x, positions, norm_mix_pre, norm_mix_post, norm_ffn_pre, norm_ffn_post, ret_w_in, ret_gn_g, ret_w_out, mla_w_in, mla_g_q, mla_g_kv, mla_w_uq, mla_w_ukv, mla_w_out, ffn_w_gu, ffn_w_down):
    batch, seq, d = x.shape
    depth = norm_mix_pre.shape[0]
    t = batch * seq
    xf = x.reshape(t, d)
    pos_col = positions.reshape(t, 1)
    cos_r, sin_r = _rope_tables(pos_col, RET_QK_DIM)
    tcos_m, tsin_m = _mla_rope_tables(pos_col)
    ret_heads = d // RET_QK_DIM
    log_gamma = jnp.log1p(-jnp.exp2(-5.0 - jnp.arange(ret_heads, dtype=F32)))

    def row(a, i):
        return a[i].reshape(1, -1)

    h = _prenorm(xf, row(norm_mix_pre, 0))
    for i in range(depth):
        j = i // N_MIXERS
        g_ffn_pre = row(norm_ffn_pre, i)
        g_next_mix = row(norm_mix_pre, i + 1) if i + 1 < depth else None
        if i % N_MIXERS == 0:
            xf, h = _retention_layer(xf, h, row(norm_mix_post, i), g_ffn_pre, ret_w_in, row(ret_gn_g, j),
                                     ret_w_out, j, cos_r, sin_r, log_gamma, batch=batch, seq=seq)
        else:
            xf, h = _mla_layer(xf, h, row(norm_mix_post, i), g_ffn_pre, mla_w_in, row(mla_g_q, j),
                               row(mla_g_kv, j), mla_w_uq, mla_w_ukv, mla_w_out, j, tcos_m, tsin_m,
                               batch=batch, seq=seq)
        xf, h = _ffn(xf, h, row(norm_ffn_post, i), g_next_mix, ffn_w_gu, ffn_w_down, i)
    return xf.reshape(batch, seq, d)
```

```python
import functools

import jax
import jax.numpy as jnp
from jax import lax
from jax.experimental import pallas as pl
from jax.experimental.pallas import tpu as pltpu

F32 = jnp.float32
BF16 = jnp.bfloat16

RET_QK_DIM = 256
RET_V_DIM = 2 * RET_QK_DIM
MLA_NOPE = 128
MLA_ROPE = 64
MLA_V = 128
MLA_QD = MLA_NOPE + MLA_ROPE
MLA_PAD = 256
LANES = 128
ROPE_THETA = 10000.0
NORM_EPS = 1e-6
N_MIXERS = 2

V7X_VMEM_BYTES = 64 * 1024 * 1024
VMEM_LIMIT_BYTES = V7X_VMEM_BYTES - 8 * 1024 * 1024

ROW_TILE = 1024
FFN_ROW_TILE = 2048
ROW_PARTS = 2
PROJ_COL_TILE = 1024
FFN_COL_TILE = 512
OUT_ROW_TILE = 512
FFN_DOWN_K_PARTS = 2
RET_CHUNK = 256
RET_SEQ_BLOCK = 1024
RET_HEADS_PER_STEP = 2
MLA_HEADS_PER_STEP = 4
ATTN_TILE = 512
ATTN_HEADS_PER_STEP = 2
ATTN_STEPS_PER_REGION = 2
OUT_COL_CHUNK = 512
NEG_BIG = -1e30


def _params(*sem):
    return pltpu.CompilerParams(dimension_semantics=sem, vmem_limit_bytes=VMEM_LIMIT_BYTES)


def _rms(x, g):
    return x * lax.rsqrt(jnp.mean(x * x, axis=-1, keepdims=True) + NORM_EPS) * g


def _bdot(a, b):
    return jnp.dot(a, b, preferred_element_type=F32)


def _silu(x):
    return x * jax.nn.sigmoid(x)


def _rope_table_kernel(pos_ref, inv_ref, cos_ref, sin_ref):
    ang = pos_ref[...].astype(F32) * inv_ref[...]
    cos_ref[...] = jnp.cos(ang)
    sin_ref[...] = jnp.sin(ang)


def _rope_tables(pos_col, d):
    t = pos_col.shape[0]
    f = d // 2
    inv = (ROPE_THETA ** (-jnp.arange(0, d, 2, dtype=F32) / d)).reshape(1, f)
    tm = ROW_TILE
    return pl.pallas_call(
        _rope_table_kernel,
        out_shape=(jax.ShapeDtypeStruct((t, f), F32), jax.ShapeDtypeStruct((t, f), F32)),
        grid=(t // tm,),
        in_specs=[pl.BlockSpec((tm, 1), lambda i: (i, 0)),
                  pl.BlockSpec((1, f), lambda i: (0, 0))],
        out_specs=(pl.BlockSpec((tm, f), lambda i: (i, 0)),
                   pl.BlockSpec((tm, f), lambda i: (i, 0))),
        compiler_params=_params("parallel"),
        name=f"rope_tables_{d}",
    )(pos_col, inv)


def _prenorm_kernel(x_ref, g_ref, h_ref):
    h_ref[...] = _rms(x_ref[...], g_ref[...]).astype(h_ref.dtype)


def _prenorm(x, g):
    t, d = x.shape
    tm = ROW_TILE // 2
    return pl.pallas_call(
        _prenorm_kernel,
        out_shape=jax.ShapeDtypeStruct((t, d), BF16),
        grid=(t // tm,),
        in_specs=[pl.BlockSpec((tm, d), lambda i: (i, 0)),
                  pl.BlockSpec((1, d), lambda i: (0, 0))],
        out_specs=pl.BlockSpec((tm, d), lambda i: (i, 0)),
        compiler_params=_params("parallel"),
        name="prenorm",
    )(x, g)


def _row_parts(n_rows):
    part = n_rows // ROW_PARTS
    return [slice(r * part, (r + 1) * part) for r in range(ROW_PARTS)]


def _cast_weight_once(w_ref, wb_ref):
    @pl.when(pl.program_id(1) == 0)
    def _():
        wb_ref[...] = w_ref[...].astype(BF16)


def _proj_plain_kernel(h_ref, w_ref, *rest, activation, with_side_cast):
    if with_side_cast:
        side_ref, o_ref, side_o_ref, wb_ref = rest
        side_o_ref[...] = side_ref[...].astype(side_o_ref.dtype)
    else:
        o_ref, wb_ref = rest
    _cast_weight_once(w_ref, wb_ref)
    for rows in _row_parts(h_ref.shape[0]):
        acc = _bdot(h_ref[rows, :], wb_ref[...])
        if activation is not None:
            acc = activation(acc)
        o_ref[rows, :] = acc.astype(o_ref.dtype)


def _proj_rope_kernel(h_ref, w_ref, cos_ref, sin_ref, o_ref, wb_ref, *, head_dim, scale):
    _cast_weight_once(w_ref, wb_ref)
    half = head_dim // 2
    for rows in _row_parts(h_ref.shape[0]):
        acc = _bdot(h_ref[rows, :], wb_ref[...])
        cos, sin = cos_ref[rows, :], sin_ref[rows, :]
        for c0 in range(0, acc.shape[1], head_dim):
            x1, x2 = acc[:, c0:c0 + half], acc[:, c0 + half:c0 + head_dim]
            r1 = x1 * cos - x2 * sin
            r2 = x1 * sin + x2 * cos
            if scale is not None:
                r1, r2 = r1 * scale, r2 * scale
            o_ref[rows, c0:c0 + half] = r1.astype(o_ref.dtype)
            o_ref[rows, c0 + half:c0 + head_dim] = r2.astype(o_ref.dtype)


def _proj_swiglu_kernel(h_ref, wg_ref, wu_ref, side_ref, o_ref, side_o_ref, wgb_ref, wub_ref):
    side_o_ref[...] = side_ref[...].astype(side_o_ref.dtype)
    _cast_weight_once(wg_ref, wgb_ref)
    _cast_weight_once(wu_ref, wub_ref)
    for rows in _row_parts(h_ref.shape[0]):
        h = h_ref[rows, :]
        gate = _bdot(h, wgb_ref[...])
        up = _bdot(h, wub_ref[...])
        o_ref[rows, :] = (_silu(gate) * up).astype(o_ref.dtype)


def _side_cast_specs(side_w, layer, n_steps, n_inner):
    _, kdim, n = side_w.shape
    rows = kdim // n_steps
    assert rows * n_steps == kdim and rows % 16 == 0, (kdim, n_steps)
    in_spec = pl.BlockSpec((None, rows, n), lambda j, i: (layer, j * n_inner + i, 0))
    out_spec = pl.BlockSpec((rows, n), lambda j, i: (j * n_inner + i, 0))
    return in_spec, out_spec, jax.ShapeDtypeStruct((kdim, n), BF16)


def _proj(h, w, layer, *, col0, n_cols, tn, out_dtype, name, activation=None, side_w=None):
    t, kdim = h.shape
    tm = ROW_TILE
    tn = min(tn, n_cols)
    off = col0 // tn
    grid = (n_cols // tn, t // tm)
    operands = [h, w]
    in_specs = [pl.BlockSpec((tm, kdim), lambda j, i: (i, 0)),
                pl.BlockSpec((None, kdim, tn), lambda j, i: (layer, 0, j + off))]
    out_shape = jax.ShapeDtypeStruct((t, n_cols), out_dtype)
    out_specs = pl.BlockSpec((tm, tn), lambda j, i: (i, j))
    if side_w is not None:
        side_in, side_out, side_shape = _side_cast_specs(side_w, layer, grid[0] * grid[1], grid[1])
        operands.append(side_w)
        in_specs.append(side_in)
        out_shape, out_specs = (out_shape, side_shape), (out_specs, side_out)
    return pl.pallas_call(
        functools.partial(_proj_plain_kernel, activation=activation, with_side_cast=side_w is not None),
        out_shape=out_shape,
        grid=grid,
        in_specs=in_specs,
        out_specs=out_specs,
        scratch_shapes=[pltpu.VMEM((kdim, tn), BF16)],
        compiler_params=_params("arbitrary", "arbitrary"),
        name=name,
    )(*operands)


def _proj_rope(h, w, layer, cos, sin, *, col0, n_cols, tn, head_dim, scale, out_dtype, name):
    t, kdim = h.shape
    tm = ROW_TILE
    tn = min(tn, n_cols)
    off = col0 // tn
    kern = functools.partial(_proj_rope_kernel, head_dim=head_dim, scale=scale)
    return pl.pallas_call(
        kern,
        out_shape=jax.ShapeDtypeStruct((t, n_cols), out_dtype),
        grid=(n_cols // tn, t // tm),
        in_specs=[pl.BlockSpec((tm, kdim), lambda j, i: (i, 0)),
                  pl.BlockSpec((None, kdim, tn), lambda j, i: (layer, 0, j + off)),
                  pl.BlockSpec((tm, head_dim // 2), lambda j, i: (i, 0)),
                  pl.BlockSpec((tm, head_dim // 2), lambda j, i: (i, 0))],
        out_specs=pl.BlockSpec((tm, tn), lambda j, i: (i, j)),
        scratch_shapes=[pltpu.VMEM((kdim, tn), BF16)],
        compiler_params=_params("arbitrary", "arbitrary"),
        name=name,
    )(h, w, cos, sin)


def _proj_swiglu(h, w_gu, layer, side_w, *, tn, name):
    t, kdim = h.shape
    d_ff = w_gu.shape[2] // 2
    tm = FFN_ROW_TILE
    n_tiles = d_ff // tn
    grid = (n_tiles, t // tm)
    side_in, side_out, side_shape = _side_cast_specs(side_w, layer, grid[0] * grid[1], grid[1])
    return pl.pallas_call(
        _proj_swiglu_kernel,
        out_shape=(jax.ShapeDtypeStruct((t, d_ff), BF16), side_shape),
        grid=grid,
        in_specs=[pl.BlockSpec((tm, kdim), lambda j, i: (i, 0)),
                  pl.BlockSpec((None, kdim, tn), lambda j, i: (layer, 0, j)),
                  pl.BlockSpec((None, kdim, tn), lambda j, i: (layer, 0, j + n_tiles)),
                  side_in],
        out_specs=(pl.BlockSpec((tm, tn), lambda j, i: (i, j)), side_out),
        scratch_shapes=[pltpu.VMEM((kdim, tn), BF16), pltpu.VMEM((kdim, tn), BF16)],
        compiler_params=_params("arbitrary", "arbitrary"),
        name=name,
    )(h, w_gu, w_gu, side_w)


def _out_proj_kernel(a_ref, w_ref, x_ref, g_ref, *rest, n_k, emit_next):
    if emit_next:
        gn_ref, o_ref, h_ref = rest
    else:
        (o_ref,) = rest
    k = pl.program_id(1)
    tk = a_ref.shape[1]

    def accumulate(kk, rows):
        a = a_ref[rows, :]
        for n in range(0, o_ref.shape[1], OUT_COL_CHUNK):
            cols = slice(n, n + OUT_COL_CHUNK)
            part = _bdot(a, w_ref[kk * tk:(kk + 1) * tk, cols])
            if kk == 0:
                o_ref[rows, cols] = part
            else:
                o_ref[rows, cols] += part

    def epilogue(rows):
        x_new = x_ref[rows, :] + _rms(o_ref[rows, :], g_ref[...])
        o_ref[rows, :] = x_new
        if emit_next:
            h_ref[rows, :] = _rms(x_new, gn_ref[...]).astype(h_ref.dtype)

    for kk in range(n_k):
        @pl.when(k == kk)
        def _(kk=kk):
            for rows in _row_parts(a_ref.shape[0]):
                accumulate(kk, rows)
                if kk == n_k - 1:
                    epilogue(rows)


def _out_proj(a, w_bf16, x, g_post, g_next, *, n_k, name):
    t, kdim = a.shape
    d = w_bf16.shape[1]
    tm = OUT_ROW_TILE
    tk = kdim // n_k
    emit_next = g_next is not None
    kern = functools.partial(_out_proj_kernel, n_k=n_k, emit_next=emit_next)
    row_spec = pl.BlockSpec((tm, d), lambda i, k: (i, 0))
    gain_spec = pl.BlockSpec((1, d), lambda i, k: (0, 0))
    operands = [a, w_bf16, x, g_post]
    in_specs = [pl.BlockSpec((tm, tk), lambda i, k: (i, k)),
                pl.BlockSpec((kdim, d), lambda i, k: (0, 0), pipeline_mode=pl.Buffered(1)),
                row_spec, gain_spec]
    out_shape = [jax.ShapeDtypeStruct((t, d), F32)]
    out_specs = [row_spec]
    if emit_next:
        operands.append(g_next)
        in_specs.append(gain_spec)
        out_shape.append(jax.ShapeDtypeStruct((t, d), BF16))
        out_specs.append(row_spec)
    res = pl.pallas_call(
        kern,
        out_shape=tuple(out_shape),
        grid=(t // tm, n_k),
        in_specs=in_specs,
        out_specs=tuple(out_specs),
        compiler_params=_params("arbitrary", "arbitrary"),
        name=name,
    )(*operands)
    return (res[0], res[1]) if emit_next else (res[0], None)


def _retention_kernel(lg_ref, q_ref, k_ref, v_ref, g_ref, gn_ref, o_ref, state_ref, *, heads_per_step):
    c_len = RET_CHUNK
    dk, dv = RET_QK_DIM, RET_V_DIM
    n_chunks = q_ref.shape[0] // c_len

    @pl.when(pl.program_id(2) == 0)
    def _():
        state_ref[...] = jnp.zeros_like(state_ref)

    row = lax.broadcasted_iota(jnp.int32, (c_len, c_len), 0)
    col = lax.broadcasted_iota(jnp.int32, (c_len, c_len), 1)
    rel = (row - col).astype(F32)
    causal = rel >= 0
    rel_pos = jnp.where(causal, rel, 0.0)
    idx = lax.broadcasted_iota(jnp.int32, (c_len, 1), 0).astype(F32)
    decays = []
    for g in range(heads_per_step):
        lg = jnp.full((1, 1), lg_ref[pl.program_id(1) * heads_per_step + g], F32)
        decays.append((jnp.where(causal, jnp.exp(lg * rel_pos), 0.0),
                       jnp.exp(lg * (idx + 1.0)),
                       jnp.exp(lg * (c_len - 1.0 - idx)),
                       jnp.exp(lg * c_len)))

    for c in range(n_chunks):
        rows = slice(c * c_len, (c + 1) * c_len)
        for g in range(heads_per_step):
            decay_in, decay_q, decay_k, decay_chunk = decays[g]
            qk_cols = slice(g * dk, (g + 1) * dk)
            v_cols = slice(g * dv, (g + 1) * dv)
            q = q_ref[rows, qk_cols]
            k = k_ref[rows, qk_cols]
            v = v_ref[rows, v_cols]
            scores = lax.dot_general(q, k.astype(BF16), (((1,), (1,)), ((), ())),
                                     preferred_element_type=F32) * decay_in
            state = state_ref[g]
            y = _bdot(scores.astype(BF16), v) + _bdot(q, state.astype(BF16)) * decay_q
            kd_t = (k * decay_k).T.astype(BF16)
            state_ref[g] = state * decay_chunk + _bdot(kd_t, v)

            mu = jnp.mean(y, axis=-1, keepdims=True)
            yc = y - mu
            var = jnp.mean(yc * yc, axis=-1, keepdims=True)
            yn = yc * lax.rsqrt(var + NORM_EPS) * gn_ref[:, v_cols]
            o_ref[rows, v_cols] = (g_ref[rows, v_cols] * yn).astype(o_ref.dtype)


def _retention_core(q, k, v, gate, gn_g, log_gamma, *, batch, seq):
    t = q.shape[0]
    dk, dv = RET_QK_DIM, RET_V_DIM
    heads = v.shape[1] // dv
    hp = RET_HEADS_PER_STEP
    lb = RET_SEQ_BLOCK
    nb = seq // lb
    kern = functools.partial(_retention_kernel, heads_per_step=hp)
    return pl.pallas_call(
        kern,
        out_shape=jax.ShapeDtypeStruct((t, heads * dv), BF16),
        grid=(batch, heads // hp, nb),
        in_specs=[pl.BlockSpec(memory_space=pltpu.SMEM),
                  pl.BlockSpec((lb, hp * dk), lambda b, h, s: (b * nb + s, h)),
                  pl.BlockSpec((lb, hp * dk), lambda b, h, s: (b * nb + s, h)),
                  pl.BlockSpec((lb, hp * dv), lambda b, h, s: (b * nb + s, h)),
                  pl.BlockSpec((lb, hp * dv), lambda b, h, s: (b * nb + s, h)),
                  pl.BlockSpec((1, hp * dv), lambda b, h, s: (0, h))],
        out_specs=pl.BlockSpec((lb, hp * dv), lambda b, h, s: (b * nb + s, h)),
        scratch_shapes=[pltpu.VMEM((hp, dk, dv), F32)],
        compiler_params=_params("parallel", "parallel", "arbitrary"),
        name="retention_core",
    )(log_gamma, q, k, v, gate, gn_g)


def _mla_rope_table_kernel(pos_ref, inv_ref, cos_mask_ref, sin_sign_ref, tcos_ref, tsin_ref):
    ang = pos_ref[...].astype(F32) * inv_ref[...]
    tcos_ref[...] = jnp.cos(ang) * cos_mask_ref[...]
    tsin_ref[...] = jnp.sin(ang) * sin_sign_ref[...]


def _mla_rope_tables(pos_col):
    t = pos_col.shape[0]
    half = MLA_ROPE // 2
    inv = ROPE_THETA ** (-jnp.arange(0, MLA_ROPE, 2, dtype=F32) / MLA_ROPE)
    ones, zeros = jnp.ones((half,), F32), jnp.zeros((half,), F32)
    inv_row = jnp.concatenate([inv, inv, zeros, zeros]).reshape(1, LANES)
    cos_mask = jnp.concatenate([ones, ones, zeros, zeros]).reshape(1, LANES)
    sin_sign = jnp.concatenate([-ones, ones, zeros, zeros]).reshape(1, LANES)
    tm = ROW_TILE
    row = pl.BlockSpec((1, LANES), lambda i: (0, 0))
    return pl.pallas_call(
        _mla_rope_table_kernel,
        out_shape=(jax.ShapeDtypeStruct((t, LANES), F32), jax.ShapeDtypeStruct((t, LANES), F32)),
        grid=(t // tm,),
        in_specs=[pl.BlockSpec((tm, 1), lambda i: (i, 0)), row, row, row],
        out_specs=(pl.BlockSpec((tm, LANES), lambda i: (i, 0)),
                   pl.BlockSpec((tm, LANES), lambda i: (i, 0))),
        compiler_params=_params("parallel"),
        name="mla_rope_tables",
    )(pos_col, inv_row, cos_mask, sin_sign)


def _mla_in_kernel(h_ref, w_ref, gq_ref, gkv_ref, tcos_ref, tsin_ref,
                   cq_ref, ckv_ref, kr_ref, wb_ref, *, q_lora, kv_lora):
    @pl.when(pl.program_id(0) == 0)
    def _():
        wb_ref[...] = w_ref[...].astype(BF16)

    for rows in _row_parts(h_ref.shape[0]):
        c = _bdot(h_ref[rows, :], wb_ref[...])
        cq_ref[rows, :] = _rms(c[:, :q_lora], gq_ref[...]).astype(cq_ref.dtype)
        ckv_ref[rows, :] = _rms(c[:, q_lora:q_lora + kv_lora], gkv_ref[...]).astype(ckv_ref.dtype)
        kr = c[:, q_lora + kv_lora:]
        half = kr.shape[1] // 2
        v = jnp.concatenate([kr, jnp.zeros_like(kr)], axis=-1)
        lane = lax.broadcasted_iota(jnp.int32, v.shape, 1)
        swapped = jnp.where(lane < half, pltpu.roll(v, LANES - half, 1), pltpu.roll(v, half, 1))
        swapped = jnp.where(lane < 2 * half, swapped, 0.0)
        kr_ref[rows, :] = v * tcos_ref[rows, :] + swapped * tsin_ref[rows, :]


def _mla_in(h, w, layer, gq, gkv, tcos, tsin):
    t, d = h.shape
    q_lora, kv_lora = gq.shape[1], gkv.shape[1]
    n_in = w.shape[2]
    tm = ROW_TILE
    kern = functools.partial(_mla_in_kernel, q_lora=q_lora, kv_lora=kv_lora)
    return pl.pallas_call(
        kern,
        out_shape=(jax.ShapeDtypeStruct((t, q_lora), BF16),
                   jax.ShapeDtypeStruct((t, kv_lora), BF16),
                   jax.ShapeDtypeStruct((t, LANES), F32)),
        grid=(t // tm,),
        in_specs=[pl.BlockSpec((tm, d), lambda i: (i, 0)),
                  pl.BlockSpec((None, d, n_in), lambda i: (layer, 0, 0)),
                  pl.BlockSpec((1, q_lora), lambda i: (0, 0)),
                  pl.BlockSpec((1, kv_lora), lambda i: (0, 0)),
                  pl.BlockSpec((tm, LANES), lambda i: (i, 0)),
                  pl.BlockSpec((tm, LANES), lambda i: (i, 0))],
        out_specs=(pl.BlockSpec((tm, q_lora), lambda i: (i, 0)),
                   pl.BlockSpec((tm, kv_lora), lambda i: (i, 0)),
                   pl.BlockSpec((tm, LANES), lambda i: (i, 0))),
        scratch_shapes=[pltpu.VMEM((d, n_in), BF16)],
        compiler_params=_params("arbitrary"),
        name="mla_in",
    )(h, w, gq, gkv, tcos, tsin)


def _mla_q_weight_kernel(w_ref, o_ref):
    half = MLA_ROPE // 2
    w = w_ref[...]
    for t in range(o_ref.shape[0]):
        o = t * MLA_QD
        nope = w[:, o:o + MLA_NOPE]
        x1 = w[:, o + MLA_NOPE:o + MLA_NOPE + half]
        x2 = w[:, o + MLA_NOPE + half:o + MLA_QD]
        o_ref[t] = jnp.concatenate([nope, x1, x2, x2, x1], axis=-1).astype(o_ref.dtype)


def _mla_q_weights(w_uq, layer):
    q_lora = w_uq.shape[1]
    heads = w_uq.shape[2] // MLA_QD
    hp = 2
    return pl.pallas_call(
        _mla_q_weight_kernel,
        out_shape=jax.ShapeDtypeStruct((heads, q_lora, MLA_PAD), BF16),
        grid=(heads // hp,),
        in_specs=[pl.BlockSpec((None, q_lora, hp * MLA_QD), lambda h: (layer, 0, h))],
        out_specs=pl.BlockSpec((hp, q_lora, MLA_PAD), lambda h: (h, 0, 0)),
        compiler_params=_params("parallel"),
        name="mla_q_weights",
    )(w_uq)


def _mla_q_kernel(cq_ref, w_ref, tcos_ref, tsin_ref, o_ref, *, scale):
    cq = cq_ref[...]
    tcos, tsin = tcos_ref[...], tsin_ref[...]
    for g in range(o_ref.shape[0]):
        res = _bdot(cq, w_ref[g])
        rot = res[:, MLA_NOPE:]
        rope = rot * tcos + pltpu.roll(rot, LANES // 2, 1) * tsin
        o_ref[g, :, :MLA_NOPE] = (res[:, :MLA_NOPE] * scale).astype(o_ref.dtype)
        o_ref[g, :, MLA_NOPE:] = (rope * scale).astype(o_ref.dtype)


def _mla_q(cq, w_q, tcos, tsin):
    t, q_lora = cq.shape
    heads = w_q.shape[0]
    tm = ROW_TILE
    hp = MLA_HEADS_PER_STEP
    kern = functools.partial(_mla_q_kernel, scale=MLA_QD ** -0.5)
    return pl.pallas_call(
        kern,
        out_shape=jax.ShapeDtypeStruct((heads, t, MLA_PAD), BF16),
        grid=(t // tm, heads // hp),
        in_specs=[pl.BlockSpec((tm, q_lora), lambda i, h: (i, 0)),
                  pl.BlockSpec((hp, q_lora, MLA_PAD), lambda i, h: (h, 0, 0)),
                  pl.BlockSpec((tm, LANES), lambda i, h: (i, 0)),
                  pl.BlockSpec((tm, LANES), lambda i, h: (i, 0))],
        out_specs=pl.BlockSpec((hp, tm, MLA_PAD), lambda i, h: (h, i, 0)),
        compiler_params=_params("parallel", "parallel"),
        name="mla_q",
    )(cq, w_q, tcos, tsin)


def _mla_kv_kernel(ckv_ref, w_ref, kr_ref, side_ref, k_ref, v_ref, side_o_ref):
    side_o_ref[...] = side_ref[...].astype(side_o_ref.dtype)
    ckv = ckv_ref[...]
    per_head = MLA_NOPE + MLA_V
    for g in range(k_ref.shape[0]):
        kv = _bdot(ckv, w_ref[:, g * per_head:(g + 1) * per_head].astype(BF16))
        k_ref[g, :, :MLA_NOPE] = kv[:, :MLA_NOPE].astype(k_ref.dtype)
        k_ref[g, :, MLA_NOPE:] = kr_ref[...].astype(k_ref.dtype)
        v_ref[g, :, :MLA_V] = kv[:, MLA_NOPE:].astype(v_ref.dtype)
        v_ref[g, :, MLA_V:] = jnp.ones((v_ref.shape[1], v_ref.shape[2] - MLA_V), v_ref.dtype)


def _mla_kv(ckv, w_ukv, layer, k_rope, side_w):
    t, kv_lora = ckv.shape
    per_head = MLA_NOPE + MLA_V
    heads = w_ukv.shape[2] // per_head
    tm = ROW_TILE
    hp = MLA_HEADS_PER_STEP
    grid = (t // tm, heads // hp)
    side_in, side_out, side_shape = _side_cast_specs(side_w, layer, grid[0] * grid[1], grid[1])
    return pl.pallas_call(
        _mla_kv_kernel,
        out_shape=(jax.ShapeDtypeStruct((heads, t, MLA_PAD), BF16),
                   jax.ShapeDtypeStruct((heads, t, 2 * MLA_V), BF16),
                   side_shape),
        grid=grid,
        in_specs=[pl.BlockSpec((tm, kv_lora), lambda i, h: (i, 0)),
                  pl.BlockSpec((None, kv_lora, hp * per_head), lambda i, h: (layer, 0, h)),
                  pl.BlockSpec((tm, LANES), lambda i, h: (i, 0)),
                  side_in],
        out_specs=(pl.BlockSpec((hp, tm, MLA_PAD), lambda i, h: (h, i, 0)),
                   pl.BlockSpec((hp, tm, 2 * MLA_V), lambda i, h: (h, i, 0)),
                   side_out),
        compiler_params=_params("arbitrary", "arbitrary"),
        name="mla_kv",
    )(ckv, w_ukv, k_rope, side_w)


def _flash_kernel(q_ref, k_ref, v_ref, o_ref, sa_ref, sb_ref, m_ref, acc_ref):
    n_heads, seq, _ = q_ref.shape
    tile = ATTN_TILE
    nq = seq // tile
    tiles = [(qi, ki) for qi in range(nq) for ki in range(qi + 1)]
    s_bufs = (sa_ref, sb_ref)

    def rows(i):
        return slice(i * tile, (i + 1) * tile)

    def scores_into(s_ref, qi, ki):
        for g in range(n_heads):
            s_ref[g] = lax.dot_general(q_ref[g, rows(qi), :], k_ref[g, rows(ki), :],
                                       (((1,), (1,)), ((), ())), preferred_element_type=F32)

    def consume(s_ref, qi, ki):
        slot = qi % 2
        for g in range(n_heads):
            s = s_ref[g]
            if ki == qi:
                row = lax.broadcasted_iota(jnp.int32, s.shape, 0)
                col = lax.broadcasted_iota(jnp.int32, s.shape, 1)
                s = jnp.where(col <= row, s, NEG_BIG)
            m_cur = jnp.max(s, axis=-1, keepdims=True)
            if ki == 0:
                m_new = m_cur
            else:
                m_prev = m_ref[slot, g]
                m_new = jnp.maximum(m_prev, m_cur)
                alpha = jnp.exp(m_prev - m_new)
            p = jnp.exp(s - m_new).astype(BF16)
            pv = _bdot(p, v_ref[g, rows(ki), :])
            acc = pv if ki == 0 else alpha * acc_ref[slot, g] + pv
            if ki == qi:
                o_ref[rows(qi), g * MLA_V:(g + 1) * MLA_V] = (acc[:, :MLA_V] / acc[:, MLA_V:]).astype(o_ref.dtype)
            else:
                acc_ref[slot, g] = acc
                m_ref[slot, g] = m_new

    def step(n):
        qi, ki = tiles[n]
        if n + 1 < len(tiles):
            scores_into(s_bufs[(n + 1) % 2], *tiles[n + 1])
        consume(s_bufs[n % 2], qi, ki)

    def steps(first):
        for n in range(first, min(first + ATTN_STEPS_PER_REGION, len(tiles))):
            step(n)

    one_trip = jnp.minimum(pl.program_id(0) + 1, 1)
    scores_into(s_bufs[0], *tiles[0])
    for first in range(0, len(tiles), ATTN_STEPS_PER_REGION):
        lax.fori_loop(0, one_trip, lambda _, carry, first=first: (steps(first), carry)[1], 0)


def _flash_attention(q, k, v, *, batch, seq):
    heads, t, _ = q.shape
    tile = ATTN_TILE
    hp = ATTN_HEADS_PER_STEP
    head_block = pl.BlockSpec((hp, seq, MLA_PAD), lambda b, h: (h, b, 0))
    return pl.pallas_call(
        _flash_kernel,
        out_shape=jax.ShapeDtypeStruct((t, heads * MLA_V), BF16),
        grid=(batch, heads // hp),
        in_specs=[head_block, head_block, head_block],
        out_specs=pl.BlockSpec((seq, hp * MLA_V), lambda b, h: (b, h)),
        scratch_shapes=[pltpu.VMEM((hp, tile, tile), F32),
                        pltpu.VMEM((hp, tile, tile), F32),
                        pltpu.VMEM((2, hp, tile, 1), F32),
                        pltpu.VMEM((2, hp, tile, 2 * MLA_V), F32)],
        compiler_params=_params("parallel", "parallel"),
        name="mla_flash_attention",
    )(q, k, v)


def _retention_layer(x, h, g_post, g_next, w_in, gn_g, w_out, layer, cos, sin, log_gamma, *, batch, seq):
    d = x.shape[1]
    dk, dv = RET_QK_DIM, RET_V_DIM
    heads = d // dk
    n_q = heads * dk
    n_v = heads * dv
    tn = PROJ_COL_TILE
    q = _proj_rope(h, w_in, layer, cos, sin, col0=0, n_cols=n_q, tn=tn, head_dim=dk,
                   scale=None, out_dtype=BF16, name="ret_q_proj")
    k = _proj_rope(h, w_in, layer, cos, sin, col0=n_q, n_cols=n_q, tn=tn, head_dim=dk,
                   scale=dk ** -0.5, out_dtype=F32, name="ret_k_proj")
    v, w_out_b = _proj(h, w_in, layer, col0=2 * n_q, n_cols=n_v, tn=tn, out_dtype=BF16,
                       name="ret_v_proj", side_w=w_out)
    gate = _proj(h, w_in, layer, col0=2 * n_q + n_v, n_cols=n_v, tn=tn, out_dtype=BF16,
                 name="ret_gate_proj", activation=_silu)
    y = _retention_core(q, k, v, gate, gn_g, log_gamma, batch=batch, seq=seq)
    return _out_proj(y, w_out_b, x, g_post, g_next, n_k=1, name="ret_out_proj")


def _mla_layer(x, h, g_post, g_next, w_in, g_q, g_kv, w_uq, w_ukv, w_out, layer, tcos, tsin, *, batch, seq):
    cq, ckv, k_rope = _mla_in(h, w_in, layer, g_q, g_kv, tcos, tsin)
    q = _mla_q(cq, _mla_q_weights(w_uq, layer), tcos, tsin)
    k, v, w_out_b = _mla_kv(ckv, w_ukv, layer, k_rope, w_out)
    o = _flash_attention(q, k, v, batch=batch, seq=seq)
    return _out_proj(o, w_out_b, x, g_post, g_next, n_k=1, name="mla_out_proj")


def _ffn(x, h, g_post, g_next, w_gu, w_down, layer):
    act, w_down_b = _proj_swiglu(h, w_gu, layer, w_down, tn=FFN_COL_TILE, name="ffn_gate_up")
    return _out_proj(act, w_down_b, x, g_post, g_next, n_k=FFN_DOWN_K_PARTS, name="ffn_down_proj")


def kernel(x, positions, norm_mix_pre, norm_mix_post, norm_ffn_pre, norm_ffn_post, ret_w_in, ret_gn_g, ret_w_out, mla_w_in, mla_g_q, mla_g_kv, mla_w_uq, mla_w_ukv, mla_w_out, ffn_w_gu, ffn_w_down):
    batch, seq, d = x.shape
    depth = norm_mix_pre.shape[0]
    t = batch * seq
    xf = x.reshape(t, d)
    pos_col = positions.reshape(t, 1)
    cos_r, sin_r = _rope_tables(pos_col, RET_QK_DIM)
    tcos_m, tsin_m = _mla_rope_tables(pos_col)
    ret_heads = d // RET_QK_DIM
    log_gamma = jnp.log1p(-jnp.exp2(-5.0 - jnp.arange(ret_heads, dtype=F32)))

    def row(a, i):
        return a[i].reshape(1, -1)

    h = _prenorm(xf, row(norm_mix_pre, 0))
    for i in range(depth):
        j = i // N_MIXERS
        g_ffn_pre = row(norm_ffn_pre, i)
        g_next_mix = row(norm_mix_pre, i + 1) if i + 1 < depth else None
        if i % N_MIXERS == 0:
            xf, h = _retention_layer(xf, h, row(norm_mix_post, i), g_ffn_pre, ret_w_in, row(ret_gn_g, j),
                                     ret_w_out, j, cos_r, sin_r, log_gamma, batch=batch, seq=seq)
        else:
            xf, h = _mla_layer(xf, h, row(norm_mix_post, i), g_ffn_pre, mla_w_in, row(mla_g_q, j),
                               row(mla_g_kv, j), mla_w_uq, mla_w_ukv, mla_w_out, j, tcos_m, tsin_m,
                               batch=batch, seq=seq)
        xf, h = _ffn(xf, h, row(norm_ffn_post, i), g_next_mix, ffn_w_gu, ffn_w_down, i)
    return xf.reshape(batch, seq, d)
```

```python
import functools

import jax
import jax.numpy as jnp
from jax import lax
from jax.experimental import pallas as pl
from jax.experimental.pallas import tpu as pltpu

F32 = jnp.float32
BF16 = jnp.bfloat16

RET_QK_DIM = 256
RET_V_DIM = 2 * RET_QK_DIM
MLA_NOPE = 128
MLA_ROPE = 64
MLA_V = 128
MLA_QD = MLA_NOPE + MLA_ROPE
MLA_PAD = 256
LANES = 128
ROPE_THETA = 10000.0
NORM_EPS = 1e-6
LOG2_E = 1.4426950408889634
N_MIXERS = 2

V7X_VMEM_BYTES = 64 * 1024 * 1024
VMEM_LIMIT_BYTES = V7X_VMEM_BYTES - 8 * 1024 * 1024

ROW_TILE = 1024
FFN_ROW_TILE = 2048
ROW_PARTS = 2
PROJ_COL_TILE = 1024
FFN_COL_TILE = 512
OUT_ROW_TILE = 512
FFN_DOWN_K_PARTS = 2
RET_CHUNK = 256
RET_SEQ_BLOCK = 1024
RET_HEADS_PER_STEP = 2
MLA_HEADS_PER_STEP = 4
MLA_Q_HEADS_PER_STEP = 8
ATTN_TILE = 512
ATTN_HEADS_PER_STEP = 2
ATTN_STEPS_PER_REGION = 2
OUT_COL_CHUNK = 512
NEG_BIG = -1e30


def _params(*sem):
    return pltpu.CompilerParams(dimension_semantics=sem, vmem_limit_bytes=VMEM_LIMIT_BYTES)


def _rms(x, g):
    return x * lax.rsqrt(jnp.mean(x * x, axis=-1, keepdims=True) + NORM_EPS) * g


def _bdot(a, b):
    return jnp.dot(a, b, preferred_element_type=F32)


def _silu(x):
    return x * jax.nn.sigmoid(x)


def _rope_table_kernel(pos_ref, inv_ref, cos_ref, sin_ref):
    ang = pos_ref[...].astype(F32) * inv_ref[...]
    cos_ref[...] = jnp.cos(ang)
    sin_ref[...] = jnp.sin(ang)


def _rope_tables(pos_col, d):
    t = pos_col.shape[0]
    f = d // 2
    inv = (ROPE_THETA ** (-jnp.arange(0, d, 2, dtype=F32) / d)).reshape(1, f)
    tm = ROW_TILE
    return pl.pallas_call(
        _rope_table_kernel,
        out_shape=(jax.ShapeDtypeStruct((t, f), F32), jax.ShapeDtypeStruct((t, f), F32)),
        grid=(t // tm,),
        in_specs=[pl.BlockSpec((tm, 1), lambda i: (i, 0)),
                  pl.BlockSpec((1, f), lambda i: (0, 0))],
        out_specs=(pl.BlockSpec((tm, f), lambda i: (i, 0)),
                   pl.BlockSpec((tm, f), lambda i: (i, 0))),
        compiler_params=_params("parallel"),
        name=f"rope_tables_{d}",
    )(pos_col, inv)


def _prenorm_kernel(x_ref, g_ref, h_ref):
    h_ref[...] = _rms(x_ref[...], g_ref[...]).astype(h_ref.dtype)


def _prenorm(x, g):
    t, d = x.shape
    tm = ROW_TILE // 2
    return pl.pallas_call(
        _prenorm_kernel,
        out_shape=jax.ShapeDtypeStruct((t, d), BF16),
        grid=(t // tm,),
        in_specs=[pl.BlockSpec((tm, d), lambda i: (i, 0)),
                  pl.BlockSpec((1, d), lambda i: (0, 0))],
        out_specs=pl.BlockSpec((tm, d), lambda i: (i, 0)),
        compiler_params=_params("parallel"),
        name="prenorm",
    )(x, g)


def _row_parts(n_rows):
    part = n_rows // ROW_PARTS
    return [slice(r * part, (r + 1) * part) for r in range(ROW_PARTS)]


def _cast_weight_once(w_ref, wb_ref):
    @pl.when(pl.program_id(1) == 0)
    def _():
        wb_ref[...] = w_ref[...].astype(BF16)


def _proj_plain_kernel(h_ref, w_ref, *rest, activation, with_side_cast):
    if with_side_cast:
        side_ref, o_ref, side_o_ref, wb_ref = rest
        side_o_ref[...] = side_ref[...].astype(side_o_ref.dtype)
    else:
        o_ref, wb_ref = rest
    _cast_weight_once(w_ref, wb_ref)
    for rows in _row_parts(h_ref.shape[0]):
        acc = _bdot(h_ref[rows, :], wb_ref[...])
        if activation is not None:
            acc = activation(acc)
        o_ref[rows, :] = acc.astype(o_ref.dtype)


def _proj_rope_kernel(h_ref, w_ref, cos_ref, sin_ref, o_ref, wb_ref, *, head_dim, scale):
    _cast_weight_once(w_ref, wb_ref)
    half = head_dim // 2
    for rows in _row_parts(h_ref.shape[0]):
        acc = _bdot(h_ref[rows, :], wb_ref[...])
        cos, sin = cos_ref[rows, :], sin_ref[rows, :]
        for c0 in range(0, acc.shape[1], head_dim):
            x1, x2 = acc[:, c0:c0 + half], acc[:, c0 + half:c0 + head_dim]
            r1 = x1 * cos - x2 * sin
            r2 = x1 * sin + x2 * cos
            if scale is not None:
                r1, r2 = r1 * scale, r2 * scale
            o_ref[rows, c0:c0 + half] = r1.astype(o_ref.dtype)
            o_ref[rows, c0 + half:c0 + head_dim] = r2.astype(o_ref.dtype)


def _proj_swiglu_kernel(h_ref, wg_ref, wu_ref, side_ref, o_ref, side_o_ref, wgb_ref, wub_ref):
    side_o_ref[...] = side_ref[...].astype(side_o_ref.dtype)
    _cast_weight_once(wg_ref, wgb_ref)
    _cast_weight_once(wu_ref, wub_ref)
    for rows in _row_parts(h_ref.shape[0]):
        h = h_ref[rows, :]
        gate = _bdot(h, wgb_ref[...])
        up = _bdot(h, wub_ref[...])
        o_ref[rows, :] = (_silu(gate) * up).astype(o_ref.dtype)


def _side_cast_specs(side_w, layer, n_steps, n_inner):
    _, kdim, n = side_w.shape
    rows = kdim // n_steps
    assert rows * n_steps == kdim and rows % 16 == 0, (kdim, n_steps)
    in_spec = pl.BlockSpec((None, rows, n), lambda j, i: (layer, j * n_inner + i, 0))
    out_spec = pl.BlockSpec((rows, n), lambda j, i: (j * n_inner + i, 0))
    return in_spec, out_spec, jax.ShapeDtypeStruct((kdim, n), BF16)


def _proj(h, w, layer, *, col0, n_cols, tn, out_dtype, name, activation=None, side_w=None):
    t, kdim = h.shape
    tm = ROW_TILE
    tn = min(tn, n_cols)
    off = col0 // tn
    grid = (n_cols // tn, t // tm)
    operands = [h, w]
    in_specs = [pl.BlockSpec((tm, kdim), lambda j, i: (i, 0)),
                pl.BlockSpec((None, kdim, tn), lambda j, i: (layer, 0, j + off))]
    out_shape = jax.ShapeDtypeStruct((t, n_cols), out_dtype)
    out_specs = pl.BlockSpec((tm, tn), lambda j, i: (i, j))
    if side_w is not None:
        side_in, side_out, side_shape = _side_cast_specs(side_w, layer, grid[0] * grid[1], grid[1])
        operands.append(side_w)
        in_specs.append(side_in)
        out_shape, out_specs = (out_shape, side_shape), (out_specs, side_out)
    return pl.pallas_call(
        functools.partial(_proj_plain_kernel, activation=activation, with_side_cast=side_w is not None),
        out_shape=out_shape,
        grid=grid,
        in_specs=in_specs,
        out_specs=out_specs,
        scratch_shapes=[pltpu.VMEM((kdim, tn), BF16)],
        compiler_params=_params("arbitrary", "arbitrary"),
        name=name,
    )(*operands)


def _proj_rope(h, w, layer, cos, sin, *, col0, n_cols, tn, head_dim, scale, out_dtype, name):
    t, kdim = h.shape
    tm = ROW_TILE
    tn = min(tn, n_cols)
    off = col0 // tn
    kern = functools.partial(_proj_rope_kernel, head_dim=head_dim, scale=scale)
    return pl.pallas_call(
        kern,
        out_shape=jax.ShapeDtypeStruct((t, n_cols), out_dtype),
        grid=(n_cols // tn, t // tm),
        in_specs=[pl.BlockSpec((tm, kdim), lambda j, i: (i, 0)),
                  pl.BlockSpec((None, kdim, tn), lambda j, i: (layer, 0, j + off)),
                  pl.BlockSpec((tm, head_dim // 2), lambda j, i: (i, 0)),
                  pl.BlockSpec((tm, head_dim // 2), lambda j, i: (i, 0))],
        out_specs=pl.BlockSpec((tm, tn), lambda j, i: (i, j)),
        scratch_shapes=[pltpu.VMEM((kdim, tn), BF16)],
        compiler_params=_params("arbitrary", "arbitrary"),
        name=name,
    )(h, w, cos, sin)


def _proj_swiglu(h, w_gu, layer, side_w, *, tn, name):
    t, kdim = h.shape
    d_ff = w_gu.shape[2] // 2
    tm = FFN_ROW_TILE
    n_tiles = d_ff // tn
    grid = (n_tiles, t // tm)
    side_in, side_out, side_shape = _side_cast_specs(side_w, layer, grid[0] * grid[1], grid[1])
    return pl.pallas_call(
        _proj_swiglu_kernel,
        out_shape=(jax.ShapeDtypeStruct((t, d_ff), BF16), side_shape),
        grid=grid,
        in_specs=[pl.BlockSpec((tm, kdim), lambda j, i: (i, 0)),
                  pl.BlockSpec((None, kdim, tn), lambda j, i: (layer, 0, j)),
                  pl.BlockSpec((None, kdim, tn), lambda j, i: (layer, 0, j + n_tiles)),
                  side_in],
        out_specs=(pl.BlockSpec((tm, tn), lambda j, i: (i, j)), side_out),
        scratch_shapes=[pltpu.VMEM((kdim, tn), BF16), pltpu.VMEM((kdim, tn), BF16)],
        compiler_params=_params("arbitrary", "arbitrary"),
        name=name,
    )(h, w_gu, w_gu, side_w)


def _out_proj_kernel(a_ref, w_ref, x_ref, g_ref, *rest, n_k, emit_next):
    if emit_next:
        gn_ref, o_ref, h_ref = rest
    else:
        (o_ref,) = rest
    k = pl.program_id(1)
    tk = a_ref.shape[1]

    def accumulate(kk, rows):
        a = a_ref[rows, :]
        for n in range(0, o_ref.shape[1], OUT_COL_CHUNK):
            cols = slice(n, n + OUT_COL_CHUNK)
            part = _bdot(a, w_ref[kk * tk:(kk + 1) * tk, cols])
            if kk == 0:
                o_ref[rows, cols] = part
            else:
                o_ref[rows, cols] += part

    def epilogue(rows):
        x_new = x_ref[rows, :] + _rms(o_ref[rows, :], g_ref[...])
        o_ref[rows, :] = x_new
        if emit_next:
            h_ref[rows, :] = _rms(x_new, gn_ref[...]).astype(h_ref.dtype)

    for kk in range(n_k):
        @pl.when(k == kk)
        def _(kk=kk):
            for rows in _row_parts(a_ref.shape[0]):
                accumulate(kk, rows)
                if kk == n_k - 1:
                    epilogue(rows)


def _out_proj(a, w_bf16, x, g_post, g_next, *, n_k, name):
    t, kdim = a.shape
    d = w_bf16.shape[1]
    tm = OUT_ROW_TILE
    tk = kdim // n_k
    emit_next = g_next is not None
    kern = functools.partial(_out_proj_kernel, n_k=n_k, emit_next=emit_next)
    row_spec = pl.BlockSpec((tm, d), lambda i, k: (i, 0))
    gain_spec = pl.BlockSpec((1, d), lambda i, k: (0, 0))
    operands = [a, w_bf16, x, g_post]
    in_specs = [pl.BlockSpec((tm, tk), lambda i, k: (i, k)),
                pl.BlockSpec((kdim, d), lambda i, k: (0, 0), pipeline_mode=pl.Buffered(1)),
                row_spec, gain_spec]
    out_shape = [jax.ShapeDtypeStruct((t, d), F32)]
    out_specs = [row_spec]
    if emit_next:
        operands.append(g_next)
        in_specs.append(gain_spec)
        out_shape.append(jax.ShapeDtypeStruct((t, d), BF16))
        out_specs.append(row_spec)
    res = pl.pallas_call(
        kern,
        out_shape=tuple(out_shape),
        grid=(t // tm, n_k),
        in_specs=in_specs,
        out_specs=tuple(out_specs),
        compiler_params=_params("arbitrary", "arbitrary"),
        name=name,
    )(*operands)
    return (res[0], res[1]) if emit_next else (res[0], None)


def _retention_kernel(lg_ref, q_ref, k_ref, v_ref, g_ref, gn_ref, o_ref, state_ref, *, heads_per_step):
    c_len = RET_CHUNK
    dk, dv = RET_QK_DIM, RET_V_DIM
    n_chunks = q_ref.shape[0] // c_len

    @pl.when(pl.program_id(2) == 0)
    def _():
        state_ref[...] = jnp.zeros_like(state_ref)

    row = lax.broadcasted_iota(jnp.int32, (c_len, c_len), 0)
    col = lax.broadcasted_iota(jnp.int32, (c_len, c_len), 1)
    rel = (row - col).astype(F32)
    causal = rel >= 0
    rel_pos = jnp.where(causal, rel, 0.0)
    idx = lax.broadcasted_iota(jnp.int32, (c_len, 1), 0).astype(F32)
    decays = []
    for g in range(heads_per_step):
        lg = jnp.full((1, 1), lg_ref[pl.program_id(1) * heads_per_step + g], F32)
        decays.append((jnp.where(causal, jnp.exp(lg * rel_pos), 0.0),
                       jnp.exp(lg * (idx + 1.0)),
                       jnp.exp(lg * (c_len - 1.0 - idx)),
                       jnp.exp(lg * c_len)))

    for c in range(n_chunks):
        rows = slice(c * c_len, (c + 1) * c_len)
        for g in range(heads_per_step):
            decay_in, decay_q, decay_k, decay_chunk = decays[g]
            qk_cols = slice(g * dk, (g + 1) * dk)
            v_cols = slice(g * dv, (g + 1) * dv)
            q = q_ref[rows, qk_cols]
            k = k_ref[rows, qk_cols]
            v = v_ref[rows, v_cols]
            scores = lax.dot_general(q, k.astype(BF16), (((1,), (1,)), ((), ())),
                                     preferred_element_type=F32) * decay_in
            state = state_ref[g]
            y = _bdot(scores.astype(BF16), v) + _bdot(q, state.astype(BF16)) * decay_q
            kd_t = (k * decay_k).T.astype(BF16)
            state_ref[g] = state * decay_chunk + _bdot(kd_t, v)

            mu = jnp.mean(y, axis=-1, keepdims=True)
            yc = y - mu
            var = jnp.mean(yc * yc, axis=-1, keepdims=True)
            yn = yc * lax.rsqrt(var + NORM_EPS) * gn_ref[:, v_cols]
            o_ref[rows, v_cols] = (g_ref[rows, v_cols] * yn).astype(o_ref.dtype)


def _retention_core(q, k, v, gate, gn_g, log_gamma, *, batch, seq):
    t = q.shape[0]
    dk, dv = RET_QK_DIM, RET_V_DIM
    heads = v.shape[1] // dv
    hp = RET_HEADS_PER_STEP
    lb = RET_SEQ_BLOCK
    nb = seq // lb
    kern = functools.partial(_retention_kernel, heads_per_step=hp)
    return pl.pallas_call(
        kern,
        out_shape=jax.ShapeDtypeStruct((t, heads * dv), BF16),
        grid=(batch, heads // hp, nb),
        in_specs=[pl.BlockSpec(memory_space=pltpu.SMEM),
                  pl.BlockSpec((lb, hp * dk), lambda b, h, s: (b * nb + s, h)),
                  pl.BlockSpec((lb, hp * dk), lambda b, h, s: (b * nb + s, h)),
                  pl.BlockSpec((lb, hp * dv), lambda b, h, s: (b * nb + s, h)),
                  pl.BlockSpec((lb, hp * dv), lambda b, h, s: (b * nb + s, h)),
                  pl.BlockSpec((1, hp * dv), lambda b, h, s: (0, h))],
        out_specs=pl.BlockSpec((lb, hp * dv), lambda b, h, s: (b * nb + s, h)),
        scratch_shapes=[pltpu.VMEM((hp, dk, dv), F32)],
        compiler_params=_params("parallel", "parallel", "arbitrary"),
        name="retention_core",
    )(log_gamma, q, k, v, gate, gn_g)


def _mla_rope_table_kernel(pos_ref, inv_ref, cos_mask_ref, sin_sign_ref, tcos_ref, tsin_ref):
    ang = pos_ref[...].astype(F32) * inv_ref[...]
    tcos_ref[...] = jnp.cos(ang) * cos_mask_ref[...]
    tsin_ref[...] = jnp.sin(ang) * sin_sign_ref[...]


def _mla_rope_tables(pos_col):
    t = pos_col.shape[0]
    half = MLA_ROPE // 2
    inv = ROPE_THETA ** (-jnp.arange(0, MLA_ROPE, 2, dtype=F32) / MLA_ROPE)
    ones, zeros = jnp.ones((half,), F32), jnp.zeros((half,), F32)
    inv_row = jnp.concatenate([inv, inv, zeros, zeros]).reshape(1, LANES)
    cos_mask = jnp.concatenate([ones, ones, zeros, zeros]).reshape(1, LANES)
    sin_sign = jnp.concatenate([-ones, ones, zeros, zeros]).reshape(1, LANES)
    tm = ROW_TILE
    row = pl.BlockSpec((1, LANES), lambda i: (0, 0))
    return pl.pallas_call(
        _mla_rope_table_kernel,
        out_shape=(jax.ShapeDtypeStruct((t, LANES), F32), jax.ShapeDtypeStruct((t, LANES), F32)),
        grid=(t // tm,),
        in_specs=[pl.BlockSpec((tm, 1), lambda i: (i, 0)), row, row, row],
        out_specs=(pl.BlockSpec((tm, LANES), lambda i: (i, 0)),
                   pl.BlockSpec((tm, LANES), lambda i: (i, 0))),
        compiler_params=_params("parallel"),
        name="mla_rope_tables",
    )(pos_col, inv_row, cos_mask, sin_sign)


def _mla_in_kernel(h_ref, w_ref, gq_ref, gkv_ref, tcos_ref, tsin_ref,
                   cq_ref, ckv_ref, kr_ref, wb_ref, *, q_lora, kv_lora):
    @pl.when(pl.program_id(0) == 0)
    def _():
        wb_ref[...] = w_ref[...].astype(BF16)

    for rows in _row_parts(h_ref.shape[0]):
        c = _bdot(h_ref[rows, :], wb_ref[...])
        cq_ref[rows, :] = _rms(c[:, :q_lora], gq_ref[...]).astype(cq_ref.dtype)
        ckv_ref[rows, :] = _rms(c[:, q_lora:q_lora + kv_lora], gkv_ref[...]).astype(ckv_ref.dtype)
        kr = c[:, q_lora + kv_lora:]
        half = kr.shape[1] // 2
        v = jnp.concatenate([kr, jnp.zeros_like(kr)], axis=-1)
        lane = lax.broadcasted_iota(jnp.int32, v.shape, 1)
        swapped = jnp.where(lane < half, pltpu.roll(v, LANES - half, 1), pltpu.roll(v, half, 1))
        swapped = jnp.where(lane < 2 * half, swapped, 0.0)
        kr_ref[rows, :] = (v * tcos_ref[rows, :] + swapped * tsin_ref[rows, :]).astype(kr_ref.dtype)


def _mla_in(h, w, layer, gq, gkv, tcos, tsin):
    t, d = h.shape
    q_lora, kv_lora = gq.shape[1], gkv.shape[1]
    n_in = w.shape[2]
    tm = ROW_TILE
    kern = functools.partial(_mla_in_kernel, q_lora=q_lora, kv_lora=kv_lora)
    return pl.pallas_call(
        kern,
        out_shape=(jax.ShapeDtypeStruct((t, q_lora), BF16),
                   jax.ShapeDtypeStruct((t, kv_lora), BF16),
                   jax.ShapeDtypeStruct((t, LANES), BF16)),
        grid=(t // tm,),
        in_specs=[pl.BlockSpec((tm, d), lambda i: (i, 0)),
                  pl.BlockSpec((None, d, n_in), lambda i: (layer, 0, 0)),
                  pl.BlockSpec((1, q_lora), lambda i: (0, 0)),
                  pl.BlockSpec((1, kv_lora), lambda i: (0, 0)),
                  pl.BlockSpec((tm, LANES), lambda i: (i, 0)),
                  pl.BlockSpec((tm, LANES), lambda i: (i, 0))],
        out_specs=(pl.BlockSpec((tm, q_lora), lambda i: (i, 0)),
                   pl.BlockSpec((tm, kv_lora), lambda i: (i, 0)),
                   pl.BlockSpec((tm, LANES), lambda i: (i, 0))),
        scratch_shapes=[pltpu.VMEM((d, n_in), BF16)],
        compiler_params=_params("arbitrary"),
        name="mla_in",
    )(h, w, gq, gkv, tcos, tsin)


def _mla_q_weight_kernel(w_ref, o_ref):
    half = MLA_ROPE // 2
    w = w_ref[...]
    for t in range(o_ref.shape[0]):
        o = t * MLA_QD
        nope = w[:, o:o + MLA_NOPE]
        x1 = w[:, o + MLA_NOPE:o + MLA_NOPE + half]
        x2 = w[:, o + MLA_NOPE + half:o + MLA_QD]
        o_ref[t] = jnp.concatenate([nope, x1, x2, x2, x1], axis=-1).astype(o_ref.dtype)


def _mla_q_weights(w_uq, layer):
    q_lora = w_uq.shape[1]
    heads = w_uq.shape[2] // MLA_QD
    hp = 2
    return pl.pallas_call(
        _mla_q_weight_kernel,
        out_shape=jax.ShapeDtypeStruct((heads, q_lora, MLA_PAD), BF16),
        grid=(heads // hp,),
        in_specs=[pl.BlockSpec((None, q_lora, hp * MLA_QD), lambda h: (layer, 0, h))],
        out_specs=pl.BlockSpec((hp, q_lora, MLA_PAD), lambda h: (h, 0, 0)),
        compiler_params=_params("parallel"),
        name="mla_q_weights",
    )(w_uq)


def _mla_q_kernel(cq_ref, w_ref, tcos_ref, tsin_ref, o_ref, *, scale):
    cq = cq_ref[...]
    tcos, tsin = tcos_ref[...], tsin_ref[...]
    for g in range(o_ref.shape[0]):
        res = _bdot(cq, w_ref[g])
        rot = res[:, MLA_NOPE:]
        rope = rot * tcos + pltpu.roll(rot, LANES // 2, 1) * tsin
        o_ref[g, :, :MLA_NOPE] = (res[:, :MLA_NOPE] * scale).astype(o_ref.dtype)
        o_ref[g, :, MLA_NOPE:] = (rope * scale).astype(o_ref.dtype)


def _mla_q(cq, w_q, tcos, tsin):
    t, q_lora = cq.shape
    heads = w_q.shape[0]
    tm = ROW_TILE
    hp = min(MLA_Q_HEADS_PER_STEP, heads)
    kern = functools.partial(_mla_q_kernel, scale=MLA_QD ** -0.5 * LOG2_E)
    return pl.pallas_call(
        kern,
        out_shape=jax.ShapeDtypeStruct((heads, t, MLA_PAD), BF16),
        grid=(t // tm, heads // hp),
        in_specs=[pl.BlockSpec((tm, q_lora), lambda i, h: (i, 0)),
                  pl.BlockSpec((hp, q_lora, MLA_PAD), lambda i, h: (h, 0, 0)),
                  pl.BlockSpec((tm, LANES), lambda i, h: (i, 0)),
                  pl.BlockSpec((tm, LANES), lambda i, h: (i, 0))],
        out_specs=pl.BlockSpec((hp, tm, MLA_PAD), lambda i, h: (h, i, 0)),
        compiler_params=_params("parallel", "parallel"),
        name="mla_q",
    )(cq, w_q, tcos, tsin)


def _mla_kv_kernel(ckv_ref, w_ref, side_ref, k_ref, v_ref, side_o_ref):
    side_o_ref[...] = side_ref[...].astype(side_o_ref.dtype)
    ckv = ckv_ref[...]
    per_head = MLA_NOPE + MLA_V
    for g in range(k_ref.shape[0]):
        kv = _bdot(ckv, w_ref[:, g * per_head:(g + 1) * per_head].astype(BF16))
        k_ref[g] = kv[:, :MLA_NOPE].astype(k_ref.dtype)
        v_ref[g] = kv[:, MLA_NOPE:].astype(v_ref.dtype)


def _mla_kv(ckv, w_ukv, layer, side_w):
    t, kv_lora = ckv.shape
    per_head = MLA_NOPE + MLA_V
    heads = w_ukv.shape[2] // per_head
    tm = ROW_TILE
    hp = MLA_HEADS_PER_STEP
    grid = (t // tm, heads // hp)
    side_in, side_out, side_shape = _side_cast_specs(side_w, layer, grid[0] * grid[1], grid[1])
    return pl.pallas_call(
        _mla_kv_kernel,
        out_shape=(jax.ShapeDtypeStruct((heads, t, MLA_NOPE), BF16),
                   jax.ShapeDtypeStruct((heads, t, MLA_V), BF16),
                   side_shape),
        grid=grid,
        in_specs=[pl.BlockSpec((tm, kv_lora), lambda i, h: (i, 0)),
                  pl.BlockSpec((None, kv_lora, hp * per_head), lambda i, h: (layer, 0, h)),
                  side_in],
        out_specs=(pl.BlockSpec((hp, tm, MLA_NOPE), lambda i, h: (h, i, 0)),
                   pl.BlockSpec((hp, tm, MLA_V), lambda i, h: (h, i, 0)),
                   side_out),
        compiler_params=_params("arbitrary", "arbitrary"),
        name="mla_kv",
    )(ckv, w_ukv, side_w)


def _flash_kernel(q_ref, kn_ref, kr_ref, vv_ref, o_ref, k_ref, v_ref, sa_ref, sb_ref, m_ref, acc_ref):
    n_heads, seq, _ = q_ref.shape
    tile = ATTN_TILE
    nq = seq // tile
    tiles = [(qi, ki) for qi in range(nq) for ki in range(qi + 1)]
    s_bufs = (sa_ref, sb_ref)
    for g in range(n_heads):
        k_ref[g, :, :MLA_NOPE] = kn_ref[g]
        k_ref[g, :, MLA_NOPE:] = kr_ref[...]
        v_ref[g, :, :MLA_V] = vv_ref[g]
        v_ref[g, :, MLA_V:] = jnp.ones((seq, v_ref.shape[2] - MLA_V), v_ref.dtype)

    def rows(i):
        return slice(i * tile, (i + 1) * tile)

    def scores_into(s_ref, qi, ki):
        for g in range(n_heads):
            s_ref[g] = lax.dot_general(q_ref[g, rows(qi), :], k_ref[g, rows(ki), :],
                                       (((1,), (1,)), ((), ())), preferred_element_type=F32)

    def consume(s_ref, qi, ki):
        slot = qi % 2
        for g in range(n_heads):
            s = s_ref[g]
            if ki == qi:
                row = lax.broadcasted_iota(jnp.int32, s.shape, 0)
                col = lax.broadcasted_iota(jnp.int32, s.shape, 1)
                s = jnp.where(col <= row, s, NEG_BIG)
            m_cur = jnp.max(s, axis=-1, keepdims=True)
            if ki == 0:
                m_new = m_cur
            else:
                m_prev = m_ref[slot, g]
                m_new = jnp.maximum(m_prev, m_cur)
                alpha = jnp.exp2(m_prev - m_new)
            p = jnp.exp2(s - m_new).astype(BF16)
            pv = _bdot(p, v_ref[g, rows(ki), :])
            acc = pv if ki == 0 else alpha * acc_ref[slot, g] + pv
            if ki == qi:
                o_ref[rows(qi), g * MLA_V:(g + 1) * MLA_V] = (acc[:, :MLA_V] / acc[:, MLA_V:]).astype(o_ref.dtype)
            else:
                acc_ref[slot, g] = acc
                m_ref[slot, g] = m_new

    def step(n):
        qi, ki = tiles[n]
        if n + 1 < len(tiles):
            scores_into(s_bufs[(n + 1) % 2], *tiles[n + 1])
        consume(s_bufs[n % 2], qi, ki)

    def steps(first):
        for n in range(first, min(first + ATTN_STEPS_PER_REGION, len(tiles))):
            step(n)

    one_trip = jnp.minimum(pl.program_id(0) + 1, 1)
    scores_into(s_bufs[0], *tiles[0])
    for first in range(0, len(tiles), ATTN_STEPS_PER_REGION):
        lax.fori_loop(0, one_trip, lambda _, carry, first=first: (steps(first), carry)[1], 0)


def _flash_attention(q, k_nope, k_rope, v, *, batch, seq):
    heads, t, _ = q.shape
    tile = ATTN_TILE
    hp = ATTN_HEADS_PER_STEP
    narrow_block = pl.BlockSpec((hp, seq, MLA_V), lambda b, h: (h, b, 0))
    return pl.pallas_call(
        _flash_kernel,
        out_shape=jax.ShapeDtypeStruct((t, heads * MLA_V), BF16),
        grid=(batch, heads // hp),
        in_specs=[pl.BlockSpec((hp, seq, MLA_PAD), lambda b, h: (h, b, 0)),
                  narrow_block,
                  pl.BlockSpec((seq, LANES), lambda b, h: (b, 0)),
                  narrow_block],
        out_specs=pl.BlockSpec((seq, hp * MLA_V), lambda b, h: (b, h)),
        scratch_shapes=[pltpu.VMEM((hp, seq, MLA_PAD), BF16),
                        pltpu.VMEM((hp, seq, 2 * MLA_V), BF16),
                        pltpu.VMEM((hp, tile, tile), F32),
                        pltpu.VMEM((hp, tile, tile), F32),
                        pltpu.VMEM((2, hp, tile, 1), F32),
                        pltpu.VMEM((2, hp, tile, 2 * MLA_V), F32)],
        compiler_params=_params("parallel", "parallel"),
        name="mla_flash_attention",
    )(q, k_nope, k_rope, v)


def _retention_layer(x, h, g_post, g_next, w_in, gn_g, w_out, layer, cos, sin, log_gamma, *, batch, seq):
    d = x.shape[1]
    dk, dv = RET_QK_DIM, RET_V_DIM
    heads = d // dk
    n_q = heads * dk
    n_v = heads * dv
    tn = PROJ_COL_TILE
    q = _proj_rope(h, w_in, layer, cos, sin, col0=0, n_cols=n_q, tn=tn, head_dim=dk,
                   scale=None, out_dtype=BF16, name="ret_q_proj")
    k = _proj_rope(h, w_in, layer, cos, sin, col0=n_q, n_cols=n_q, tn=tn, head_dim=dk,
                   scale=dk ** -0.5, out_dtype=F32, name="ret_k_proj")
    v, w_out_b = _proj(h, w_in, layer, col0=2 * n_q, n_cols=n_v, tn=tn, out_dtype=BF16,
                       name="ret_v_proj", side_w=w_out)
    gate = _proj(h, w_in, layer, col0=2 * n_q + n_v, n_cols=n_v, tn=tn, out_dtype=BF16,
                 name="ret_gate_proj", activation=_silu)
    y = _retention_core(q, k, v, gate, gn_g, log_gamma, batch=batch, seq=seq)
    return _out_proj(y, w_out_b, x, g_post, g_next, n_k=1, name="ret_out_proj")


def _mla_layer(x, h, g_post, g_next, w_in, g_q, g_kv, w_uq, w_ukv, w_out, layer, tcos, tsin, *, batch, seq):
    cq, ckv, k_rope = _mla_in(h, w_in, layer, g_q, g_kv, tcos, tsin)
    q = _mla_q(cq, _mla_q_weights(w_uq, layer), tcos, tsin)
    k_nope, v, w_out_b = _mla_kv(ckv, w_ukv, layer, w_out)
    o = _flash_attention(q, k_nope, k_rope, v, batch=batch, seq=seq)
    return _out_proj(o, w_out_b, x, g_post, g_next, n_k=1, name="mla_out_proj")


def _ffn(x, h, g_post, g_next, w_gu, w_down, layer):
    act, w_down_b = _proj_swiglu(h, w_gu, layer, w_down, tn=FFN_COL_TILE, name="ffn_gate_up")
    return _out_proj(act, w_down_b, x, g_post, g_next, n_k=FFN_DOWN_K_PARTS, name="ffn_down_proj")


def kernel(x, positions, norm_mix_pre, norm_mix_post, norm_ffn_pre, norm_ffn_post, ret_w_in, ret_gn_g, ret_w_out, mla_w_in, mla_g_q, mla_g_kv, mla_w_uq, mla_w_ukv, mla_w_out, ffn_w_gu, ffn_w_down):
    batch, seq, d = x.shape
    depth = norm_mix_pre.shape[0]
    t = batch * seq
    xf = x.reshape(t, d)
    pos_col = positions.reshape(t, 1)
    cos_r, sin_r = _rope_tables(pos_col, RET_QK_DIM)
    tcos_m, tsin_m = _mla_rope_tables(pos_col)
    ret_heads = d // RET_QK_DIM
    log_gamma = jnp.log1p(-jnp.exp2(-5.0 - jnp.arange(ret_heads, dtype=F32)))

    def row(a, i):
        return a[i].reshape(1, -1)

    h = _prenorm(xf, row(norm_mix_pre, 0))
    for i in range(depth):
        j = i // N_MIXERS
        g_ffn_pre = row(norm_ffn_pre, i)
        g_next_mix = row(norm_mix_pre, i + 1) if i + 1 < depth else None
        if i % N_MIXERS == 0:
            xf, h = _retention_layer(xf, h, row(norm_mix_post, i), g_ffn_pre, ret_w_in, row(ret_gn_g, j),
                                     ret_w_out, j, cos_r, sin_r, log_gamma, batch=batch, seq=seq)
        else:
            xf, h = _mla_layer(xf, h, row(norm_mix_post, i), g_ffn_pre, mla_w_in, row(mla_g_q, j),
                               row(mla_g_kv, j), mla_w_uq, mla_w_ukv, mla_w_out, j, tcos_m, tsin_m,
                               batch=batch, seq=seq)
        xf, h = _ffn(xf, h, row(norm_ffn_post, i), g_next_mix, ffn_w_gu, ffn_w_down, i)
    return xf.reshape(batch, seq, d)
```

```python
import functools

import jax
import jax.numpy as jnp
from jax import lax
from jax.experimental import pallas as pl
from jax.experimental.pallas import tpu as pltpu

F32 = jnp.float32
BF16 = jnp.bfloat16

RET_QK_DIM = 256
RET_V_DIM = 2 * RET_QK_DIM
MLA_NOPE = 128
MLA_ROPE = 64
MLA_V = 128
MLA_QD = MLA_NOPE + MLA_ROPE
MLA_PAD = 256
LANES = 128
ROPE_THETA = 10000.0
NORM_EPS = 1e-6
LOG2_E = 1.4426950408889634
N_MIXERS = 2

V7X_VMEM_BYTES = 64 * 1024 * 1024
VMEM_LIMIT_BYTES = V7X_VMEM_BYTES - 8 * 1024 * 1024

ROW_TILE = 1024
FFN_ROW_TILE = 2048
ROW_PARTS = 2
PROJ_COL_TILE = 1024
FFN_COL_TILE = 512
OUT_ROW_TILE = 512
FFN_DOWN_K_PARTS = 2
RET_CHUNK = 256
RET_SEQ_BLOCK = 1024
RET_HEADS_PER_STEP = 2
MLA_HEADS_PER_STEP = 4
MLA_Q_HEADS_PER_STEP = 8
ATTN_TILE = 512
ATTN_HEADS_PER_STEP = 2
ATTN_STEPS_PER_REGION = 2
OUT_COL_CHUNK = 512
NEG_BIG = -1e30


def _params(*sem):
    return pltpu.CompilerParams(dimension_semantics=sem, vmem_limit_bytes=VMEM_LIMIT_BYTES)


def _rms(x, g):
    return x * lax.rsqrt(jnp.mean(x * x, axis=-1, keepdims=True) + NORM_EPS) * g


def _bdot(a, b):
    return jnp.dot(a, b, preferred_element_type=F32)


def _silu(x):
    return x * jax.nn.sigmoid(x)


def _rope_table_kernel(pos_ref, inv_ref, cos_ref, sin_ref):
    ang = pos_ref[...].astype(F32) * inv_ref[...]
    cos_ref[...] = jnp.cos(ang)
    sin_ref[...] = jnp.sin(ang)


def _rope_tables(pos_col, d):
    t = pos_col.shape[0]
    f = d // 2
    inv = (ROPE_THETA ** (-jnp.arange(0, d, 2, dtype=F32) / d)).reshape(1, f)
    tm = ROW_TILE
    return pl.pallas_call(
        _rope_table_kernel,
        out_shape=(jax.ShapeDtypeStruct((t, f), F32), jax.ShapeDtypeStruct((t, f), F32)),
        grid=(t // tm,),
        in_specs=[pl.BlockSpec((tm, 1), lambda i: (i, 0)),
                  pl.BlockSpec((1, f), lambda i: (0, 0))],
        out_specs=(pl.BlockSpec((tm, f), lambda i: (i, 0)),
                   pl.BlockSpec((tm, f), lambda i: (i, 0))),
        compiler_params=_params("parallel"),
        name=f"rope_tables_{d}",
    )(pos_col, inv)


def _prenorm_kernel(x_ref, g_ref, h_ref):
    h_ref[...] = _rms(x_ref[...], g_ref[...]).astype(h_ref.dtype)


def _prenorm(x, g):
    t, d = x.shape
    tm = ROW_TILE // 2
    return pl.pallas_call(
        _prenorm_kernel,
        out_shape=jax.ShapeDtypeStruct((t, d), BF16),
        grid=(t // tm,),
        in_specs=[pl.BlockSpec((tm, d), lambda i: (i, 0)),
                  pl.BlockSpec((1, d), lambda i: (0, 0))],
        out_specs=pl.BlockSpec((tm, d), lambda i: (i, 0)),
        compiler_params=_params("parallel"),
        name="prenorm",
    )(x, g)


def _row_parts(n_rows):
    part = n_rows // ROW_PARTS
    return [slice(r * part, (r + 1) * part) for r in range(ROW_PARTS)]


def _cast_weight_once(w_ref, wb_ref):
    @pl.when(pl.program_id(1) == 0)
    def _():
        wb_ref[...] = w_ref[...].astype(BF16)


def _proj_plain_kernel(h_ref, w_ref, *rest, activation, with_side_cast):
    if with_side_cast:
        side_ref, o_ref, side_o_ref, wb_ref = rest
        side_o_ref[...] = side_ref[...].astype(side_o_ref.dtype)
    else:
        o_ref, wb_ref = rest
    _cast_weight_once(w_ref, wb_ref)
    for rows in _row_parts(h_ref.shape[0]):
        acc = _bdot(h_ref[rows, :], wb_ref[...])
        if activation is not None:
            acc = activation(acc)
        o_ref[rows, :] = acc.astype(o_ref.dtype)


def _proj_rope_kernel(h_ref, w_ref, cos_ref, sin_ref, o_ref, wb_ref, *, head_dim, scale):
    _cast_weight_once(w_ref, wb_ref)
    half = head_dim // 2
    for rows in _row_parts(h_ref.shape[0]):
        acc = _bdot(h_ref[rows, :], wb_ref[...])
        cos, sin = cos_ref[rows, :], sin_ref[rows, :]
        for c0 in range(0, acc.shape[1], head_dim):
            x1, x2 = acc[:, c0:c0 + half], acc[:, c0 + half:c0 + head_dim]
            r1 = x1 * cos - x2 * sin
            r2 = x1 * sin + x2 * cos
            if scale is not None:
                r1, r2 = r1 * scale, r2 * scale
            o_ref[rows, c0:c0 + half] = r1.astype(o_ref.dtype)
            o_ref[rows, c0 + half:c0 + head_dim] = r2.astype(o_ref.dtype)


def _proj_swiglu_kernel(h_ref, wg_ref, wu_ref, side_ref, o_ref, side_o_ref, wgb_ref, wub_ref):
    side_o_ref[...] = side_ref[...].astype(side_o_ref.dtype)
    _cast_weight_once(wg_ref, wgb_ref)
    _cast_weight_once(wu_ref, wub_ref)
    for rows in _row_parts(h_ref.shape[0]):
        h = h_ref[rows, :]
        gate = _bdot(h, wgb_ref[...])
        up = _bdot(h, wub_ref[...])
        o_ref[rows, :] = (_silu(gate) * up).astype(o_ref.dtype)


def _side_cast_specs(side_w, layer, n_steps, n_inner):
    _, kdim, n = side_w.shape
    rows = kdim // n_steps
    assert rows * n_steps == kdim and rows % 16 == 0, (kdim, n_steps)
    in_spec = pl.BlockSpec((None, rows, n), lambda j, i: (layer, j * n_inner + i, 0))
    out_spec = pl.BlockSpec((rows, n), lambda j, i: (j * n_inner + i, 0))
    return in_spec, out_spec, jax.ShapeDtypeStruct((kdim, n), BF16)


def _proj(h, w, layer, *, col0, n_cols, tn, out_dtype, name, activation=None, side_w=None):
    t, kdim = h.shape
    tm = ROW_TILE
    tn = min(tn, n_cols)
    off = col0 // tn
    grid = (n_cols // tn, t // tm)
    operands = [h, w]
    in_specs = [pl.BlockSpec((tm, kdim), lambda j, i: (i, 0)),
                pl.BlockSpec((None, kdim, tn), lambda j, i: (layer, 0, j + off))]
    out_shape = jax.ShapeDtypeStruct((t, n_cols), out_dtype)
    out_specs = pl.BlockSpec((tm, tn), lambda j, i: (i, j))
    if side_w is not None:
        side_in, side_out, side_shape = _side_cast_specs(side_w, layer, grid[0] * grid[1], grid[1])
        operands.append(side_w)
        in_specs.append(side_in)
        out_shape, out_specs = (out_shape, side_shape), (out_specs, side_out)
    return pl.pallas_call(
        functools.partial(_proj_plain_kernel, activation=activation, with_side_cast=side_w is not None),
        out_shape=out_shape,
        grid=grid,
        in_specs=in_specs,
        out_specs=out_specs,
        scratch_shapes=[pltpu.VMEM((kdim, tn), BF16)],
        compiler_params=_params("arbitrary", "arbitrary"),
        name=name,
    )(*operands)


def _proj_rope(h, w, layer, cos, sin, *, col0, n_cols, tn, head_dim, scale, out_dtype, name):
    t, kdim = h.shape
    tm = ROW_TILE
    tn = min(tn, n_cols)
    off = col0 // tn
    kern = functools.partial(_proj_rope_kernel, head_dim=head_dim, scale=scale)
    return pl.pallas_call(
        kern,
        out_shape=jax.ShapeDtypeStruct((t, n_cols), out_dtype),
        grid=(n_cols // tn, t // tm),
        in_specs=[pl.BlockSpec((tm, kdim), lambda j, i: (i, 0)),
                  pl.BlockSpec((None, kdim, tn), lambda j, i: (layer, 0, j + off)),
                  pl.BlockSpec((tm, head_dim // 2), lambda j, i: (i, 0)),
                  pl.BlockSpec((tm, head_dim // 2), lambda j, i: (i, 0))],
        out_specs=pl.BlockSpec((tm, tn), lambda j, i: (i, j)),
        scratch_shapes=[pltpu.VMEM((kdim, tn), BF16)],
        compiler_params=_params("arbitrary", "arbitrary"),
        name=name,
    )(h, w, cos, sin)


def _proj_swiglu(h, w_gu, layer, side_w, *, tn, name):
    t, kdim = h.shape
    d_ff = w_gu.shape[2] // 2
    tm = FFN_ROW_TILE
    n_tiles = d_ff // tn
    grid = (n_tiles, t // tm)
    side_in, side_out, side_shape = _side_cast_specs(side_w, layer, grid[0] * grid[1], grid[1])
    return pl.pallas_call(
        _proj_swiglu_kernel,
        out_shape=(jax.ShapeDtypeStruct((t, d_ff), BF16), side_shape),
        grid=grid,
        in_specs=[pl.BlockSpec((tm, kdim), lambda j, i: (i, 0)),
                  pl.BlockSpec((None, kdim, tn), lambda j, i: (layer, 0, j)),
                  pl.BlockSpec((None, kdim, tn), lambda j, i: (layer, 0, j + n_tiles)),
                  side_in],
        out_specs=(pl.BlockSpec((tm, tn), lambda j, i: (i, j)), side_out),
        scratch_shapes=[pltpu.VMEM((kdim, tn), BF16), pltpu.VMEM((kdim, tn), BF16)],
        compiler_params=_params("arbitrary", "arbitrary"),
        name=name,
    )(h, w_gu, w_gu, side_w)


def _out_proj_kernel(a_ref, w_ref, x_ref, g_ref, *rest, n_k, emit_next):
    if emit_next:
        gn_ref, o_ref, h_ref = rest
    else:
        (o_ref,) = rest
    k = pl.program_id(1)
    tk = a_ref.shape[1]

    def accumulate(kk, rows):
        a = a_ref[rows, :]
        for n in range(0, o_ref.shape[1], OUT_COL_CHUNK):
            cols = slice(n, n + OUT_COL_CHUNK)
            part = _bdot(a, w_ref[kk * tk:(kk + 1) * tk, cols])
            if kk == 0:
                o_ref[rows, cols] = part
            else:
                o_ref[rows, cols] += part

    def epilogue(rows):
        x_new = x_ref[rows, :] + _rms(o_ref[rows, :], g_ref[...])
        o_ref[rows, :] = x_new
        if emit_next:
            h_ref[rows, :] = _rms(x_new, gn_ref[...]).astype(h_ref.dtype)

    for kk in range(n_k):
        @pl.when(k == kk)
        def _(kk=kk):
            for rows in _row_parts(a_ref.shape[0]):
                accumulate(kk, rows)
                if kk == n_k - 1:
                    epilogue(rows)


def _out_proj(a, w_bf16, x, g_post, g_next, *, n_k, name):
    t, kdim = a.shape
    d = w_bf16.shape[1]
    tm = OUT_ROW_TILE
    tk = kdim // n_k
    emit_next = g_next is not None
    kern = functools.partial(_out_proj_kernel, n_k=n_k, emit_next=emit_next)
    row_spec = pl.BlockSpec((tm, d), lambda i, k: (i, 0))
    gain_spec = pl.BlockSpec((1, d), lambda i, k: (0, 0))
    operands = [a, w_bf16, x, g_post]
    in_specs = [pl.BlockSpec((tm, tk), lambda i, k: (i, k)),
                pl.BlockSpec((kdim, d), lambda i, k: (0, 0), pipeline_mode=pl.Buffered(1)),
                row_spec, gain_spec]
    out_shape = [jax.ShapeDtypeStruct((t, d), F32)]
    out_specs = [row_spec]
    if emit_next:
        operands.append(g_next)
        in_specs.append(gain_spec)
        out_shape.append(jax.ShapeDtypeStruct((t, d), BF16))
        out_specs.append(row_spec)
    res = pl.pallas_call(
        kern,
        out_shape=tuple(out_shape),
        grid=(t // tm, n_k),
        in_specs=in_specs,
        out_specs=tuple(out_specs),
        compiler_params=_params("arbitrary", "arbitrary"),
        name=name,
    )(*operands)
    return (res[0], res[1]) if emit_next else (res[0], None)


def _retention_kernel(lg_ref, q_ref, k_ref, v_ref, h_ref, wg_ref, gn_ref, o_ref,
                      state_ref, wgb_ref, gate_ref, *, heads_per_step):
    c_len = RET_CHUNK
    dk, dv = RET_QK_DIM, RET_V_DIM
    n_chunks = q_ref.shape[0] // c_len
    first_block_of_heads = jnp.logical_and(pl.program_id(1) == 0, pl.program_id(2) == 0)

    @pl.when(first_block_of_heads)
    def _():
        wgb_ref[...] = wg_ref[...].astype(BF16)

    @pl.when(pl.program_id(2) == 0)
    def _():
        state_ref[...] = jnp.zeros_like(state_ref)

    row = lax.broadcasted_iota(jnp.int32, (c_len, c_len), 0)
    col = lax.broadcasted_iota(jnp.int32, (c_len, c_len), 1)
    rel = (row - col).astype(F32)
    causal = rel >= 0
    rel_pos = jnp.where(causal, rel, 0.0)
    idx = lax.broadcasted_iota(jnp.int32, (c_len, 1), 0).astype(F32)
    decays = []
    for g in range(heads_per_step):
        lg = jnp.full((1, 1), lg_ref[pl.program_id(0) * heads_per_step + g], F32)
        decays.append((jnp.where(causal, jnp.exp(lg * rel_pos), 0.0),
                       jnp.exp(lg * (idx + 1.0)),
                       jnp.exp(lg * (c_len - 1.0 - idx)),
                       jnp.exp(lg * c_len)))

    for rows in _row_parts(h_ref.shape[0]):
        for g in range(heads_per_step):
            v_cols = slice(g * dv, (g + 1) * dv)
            gate_ref[rows, v_cols] = _silu(_bdot(h_ref[rows, :], wgb_ref[:, v_cols])).astype(gate_ref.dtype)

    for c in range(n_chunks):
        rows = slice(c * c_len, (c + 1) * c_len)
        for g in range(heads_per_step):
            decay_in, decay_q, decay_k, decay_chunk = decays[g]
            qk_cols = slice(g * dk, (g + 1) * dk)
            v_cols = slice(g * dv, (g + 1) * dv)
            q = q_ref[rows, qk_cols]
            k = k_ref[rows, qk_cols]
            v = v_ref[rows, v_cols]
            scores = lax.dot_general(q, k.astype(BF16), (((1,), (1,)), ((), ())),
                                     preferred_element_type=F32) * decay_in
            state = state_ref[g]
            y = _bdot(scores.astype(BF16), v) + _bdot(q, state.astype(BF16)) * decay_q
            kd_t = (k * decay_k).T.astype(BF16)
            state_ref[g] = state * decay_chunk + _bdot(kd_t, v)

            mu = jnp.mean(y, axis=-1, keepdims=True)
            yc = y - mu
            var = jnp.mean(yc * yc, axis=-1, keepdims=True)
            yn = yc * lax.rsqrt(var + NORM_EPS) * gn_ref[:, v_cols]
            o_ref[rows, v_cols] = (gate_ref[rows, v_cols] * yn).astype(o_ref.dtype)


def _retention_core(q, k, v, h, w_in, layer, gate_col0, gn_g, log_gamma, *, batch, seq):
    t, d = h.shape
    dk, dv = RET_QK_DIM, RET_V_DIM
    heads = v.shape[1] // dv
    hp = RET_HEADS_PER_STEP
    lb = RET_SEQ_BLOCK
    nb = seq // lb
    gate_off = gate_col0 // (hp * dv)
    kern = functools.partial(_retention_kernel, heads_per_step=hp)
    return pl.pallas_call(
        kern,
        out_shape=jax.ShapeDtypeStruct((t, heads * dv), BF16),
        grid=(heads // hp, batch, nb),
        in_specs=[pl.BlockSpec(memory_space=pltpu.SMEM),
                  pl.BlockSpec((lb, hp * dk), lambda h_, b, s: (b * nb + s, h_)),
                  pl.BlockSpec((lb, hp * dk), lambda h_, b, s: (b * nb + s, h_)),
                  pl.BlockSpec((lb, hp * dv), lambda h_, b, s: (b * nb + s, h_)),
                  pl.BlockSpec((lb, d), lambda h_, b, s: (b * nb + s, 0)),
                  pl.BlockSpec((None, d, hp * dv), lambda h_, b, s: (layer, 0, gate_off + h_)),
                  pl.BlockSpec((1, hp * dv), lambda h_, b, s: (0, h_))],
        out_specs=pl.BlockSpec((lb, hp * dv), lambda h_, b, s: (b * nb + s, h_)),
        scratch_shapes=[pltpu.VMEM((hp, dk, dv), F32),
                        pltpu.VMEM((d, hp * dv), BF16),
                        pltpu.VMEM((lb, hp * dv), BF16)],
        compiler_params=_params("arbitrary", "arbitrary", "arbitrary"),
        name="retention_core",
    )(log_gamma, q, k, v, h, w_in, gn_g)


def _mla_rope_table_kernel(pos_ref, inv_ref, cos_mask_ref, sin_sign_ref, tcos_ref, tsin_ref):
    ang = pos_ref[...].astype(F32) * inv_ref[...]
    tcos_ref[...] = jnp.cos(ang) * cos_mask_ref[...]
    tsin_ref[...] = jnp.sin(ang) * sin_sign_ref[...]


def _mla_rope_tables(pos_col):
    t = pos_col.shape[0]
    half = MLA_ROPE // 2
    inv = ROPE_THETA ** (-jnp.arange(0, MLA_ROPE, 2, dtype=F32) / MLA_ROPE)
    ones, zeros = jnp.ones((half,), F32), jnp.zeros((half,), F32)
    inv_row = jnp.concatenate([inv, inv, zeros, zeros]).reshape(1, LANES)
    cos_mask = jnp.concatenate([ones, ones, zeros, zeros]).reshape(1, LANES)
    sin_sign = jnp.concatenate([-ones, ones, zeros, zeros]).reshape(1, LANES)
    tm = ROW_TILE
    row = pl.BlockSpec((1, LANES), lambda i: (0, 0))
    return pl.pallas_call(
        _mla_rope_table_kernel,
        out_shape=(jax.ShapeDtypeStruct((t, LANES), F32), jax.ShapeDtypeStruct((t, LANES), F32)),
        grid=(t // tm,),
        in_specs=[pl.BlockSpec((tm, 1), lambda i: (i, 0)), row, row, row],
        out_specs=(pl.BlockSpec((tm, LANES), lambda i: (i, 0)),
                   pl.BlockSpec((tm, LANES), lambda i: (i, 0))),
        compiler_params=_params("parallel"),
        name="mla_rope_tables",
    )(pos_col, inv_row, cos_mask, sin_sign)


def _mla_in_kernel(h_ref, w_ref, gq_ref, gkv_ref, tcos_ref, tsin_ref,
                   cq_ref, ckv_ref, kr_ref, wb_ref, *, q_lora, kv_lora):
    @pl.when(pl.program_id(0) == 0)
    def _():
        wb_ref[...] = w_ref[...].astype(BF16)

    for rows in _row_parts(h_ref.shape[0]):
        c = _bdot(h_ref[rows, :], wb_ref[...])
        cq_ref[rows, :] = _rms(c[:, :q_lora], gq_ref[...]).astype(cq_ref.dtype)
        ckv_ref[rows, :] = _rms(c[:, q_lora:q_lora + kv_lora], gkv_ref[...]).astype(ckv_ref.dtype)
        kr = c[:, q_lora + kv_lora:]
        half = kr.shape[1] // 2
        v = jnp.concatenate([kr, jnp.zeros_like(kr)], axis=-1)
        lane = lax.broadcasted_iota(jnp.int32, v.shape, 1)
        swapped = jnp.where(lane < half, pltpu.roll(v, LANES - half, 1), pltpu.roll(v, half, 1))
        swapped = jnp.where(lane < 2 * half, swapped, 0.0)
        kr_ref[rows, :] = (v * tcos_ref[rows, :] + swapped * tsin_ref[rows, :]).astype(kr_ref.dtype)


def _mla_in(h, w, layer, gq, gkv, tcos, tsin):
    t, d = h.shape
    q_lora, kv_lora = gq.shape[1], gkv.shape[1]
    n_in = w.shape[2]
    tm = ROW_TILE
    kern = functools.partial(_mla_in_kernel, q_lora=q_lora, kv_lora=kv_lora)
    return pl.pallas_call(
        kern,
        out_shape=(jax.ShapeDtypeStruct((t, q_lora), BF16),
                   jax.ShapeDtypeStruct((t, kv_lora), BF16),
                   jax.ShapeDtypeStruct((t, LANES), BF16)),
        grid=(t // tm,),
        in_specs=[pl.BlockSpec((tm, d), lambda i: (i, 0)),
                  pl.BlockSpec((None, d, n_in), lambda i: (layer, 0, 0)),
                  pl.BlockSpec((1, q_lora), lambda i: (0, 0)),
                  pl.BlockSpec((1, kv_lora), lambda i: (0, 0)),
                  pl.BlockSpec((tm, LANES), lambda i: (i, 0)),
                  pl.BlockSpec((tm, LANES), lambda i: (i, 0))],
        out_specs=(pl.BlockSpec((tm, q_lora), lambda i: (i, 0)),
                   pl.BlockSpec((tm, kv_lora), lambda i: (i, 0)),
                   pl.BlockSpec((tm, LANES), lambda i: (i, 0))),
        scratch_shapes=[pltpu.VMEM((d, n_in), BF16)],
        compiler_params=_params("arbitrary"),
        name="mla_in",
    )(h, w, gq, gkv, tcos, tsin)


def _mla_q_weight_kernel(w_ref, o_ref):
    half = MLA_ROPE // 2
    w = w_ref[...]
    for t in range(o_ref.shape[0]):
        o = t * MLA_QD
        nope = w[:, o:o + MLA_NOPE]
        x1 = w[:, o + MLA_NOPE:o + MLA_NOPE + half]
        x2 = w[:, o + MLA_NOPE + half:o + MLA_QD]
        o_ref[t] = jnp.concatenate([nope, x1, x2, x2, x1], axis=-1).astype(o_ref.dtype)


def _mla_q_weights(w_uq, layer):
    q_lora = w_uq.shape[1]
    heads = w_uq.shape[2] // MLA_QD
    hp = 2
    return pl.pallas_call(
        _mla_q_weight_kernel,
        out_shape=jax.ShapeDtypeStruct((heads, q_lora, MLA_PAD), BF16),
        grid=(heads // hp,),
        in_specs=[pl.BlockSpec((None, q_lora, hp * MLA_QD), lambda h: (layer, 0, h))],
        out_specs=pl.BlockSpec((hp, q_lora, MLA_PAD), lambda h: (h, 0, 0)),
        compiler_params=_params("parallel"),
        name="mla_q_weights",
    )(w_uq)


def _mla_q_kernel(cq_ref, w_ref, tcos_ref, tsin_ref, o_ref, *, scale):
    cq = cq_ref[...]
    tcos, tsin = tcos_ref[...], tsin_ref[...]
    for g in range(o_ref.shape[0]):
        res = _bdot(cq, w_ref[g])
        rot = res[:, MLA_NOPE:]
        rope = rot * tcos + pltpu.roll(rot, LANES // 2, 1) * tsin
        o_ref[g, :, :MLA_NOPE] = (res[:, :MLA_NOPE] * scale).astype(o_ref.dtype)
        o_ref[g, :, MLA_NOPE:] = (rope * scale).astype(o_ref.dtype)


def _mla_q(cq, w_q, tcos, tsin):
    t, q_lora = cq.shape
    heads = w_q.shape[0]
    tm = ROW_TILE
    hp = min(MLA_Q_HEADS_PER_STEP, heads)
    kern = functools.partial(_mla_q_kernel, scale=MLA_QD ** -0.5 * LOG2_E)
    return pl.pallas_call(
        kern,
        out_shape=jax.ShapeDtypeStruct((heads, t, MLA_PAD), BF16),
        grid=(t // tm, heads // hp),
        in_specs=[pl.BlockSpec((tm, q_lora), lambda i, h: (i, 0)),
                  pl.BlockSpec((hp, q_lora, MLA_PAD), lambda i, h: (h, 0, 0)),
                  pl.BlockSpec((tm, LANES), lambda i, h: (i, 0)),
                  pl.BlockSpec((tm, LANES), lambda i, h: (i, 0))],
        out_specs=pl.BlockSpec((hp, tm, MLA_PAD), lambda i, h: (h, i, 0)),
        compiler_params=_params("parallel", "parallel"),
        name="mla_q",
    )(cq, w_q, tcos, tsin)


def _mla_kv_kernel(ckv_ref, w_ref, side_ref, k_ref, v_ref, side_o_ref):
    side_o_ref[...] = side_ref[...].astype(side_o_ref.dtype)
    ckv = ckv_ref[...]
    per_head = MLA_NOPE + MLA_V
    for g in range(k_ref.shape[0]):
        kv = _bdot(ckv, w_ref[:, g * per_head:(g + 1) * per_head].astype(BF16))
        k_ref[g] = kv[:, :MLA_NOPE].astype(k_ref.dtype)
        v_ref[g] = kv[:, MLA_NOPE:].astype(v_ref.dtype)


def _mla_kv(ckv, w_ukv, layer, side_w):
    t, kv_lora = ckv.shape
    per_head = MLA_NOPE + MLA_V
    heads = w_ukv.shape[2] // per_head
    tm = ROW_TILE
    hp = MLA_HEADS_PER_STEP
    grid = (t // tm, heads // hp)
    side_in, side_out, side_shape = _side_cast_specs(side_w, layer, grid[0] * grid[1], grid[1])
    return pl.pallas_call(
        _mla_kv_kernel,
        out_shape=(jax.ShapeDtypeStruct((heads, t, MLA_NOPE), BF16),
                   jax.ShapeDtypeStruct((heads, t, MLA_V), BF16),
                   side_shape),
        grid=grid,
        in_specs=[pl.BlockSpec((tm, kv_lora), lambda i, h: (i, 0)),
                  pl.BlockSpec((None, kv_lora, hp * per_head), lambda i, h: (layer, 0, h)),
                  side_in],
        out_specs=(pl.BlockSpec((hp, tm, MLA_NOPE), lambda i, h: (h, i, 0)),
                   pl.BlockSpec((hp, tm, MLA_V), lambda i, h: (h, i, 0)),
                   side_out),
        compiler_params=_params("arbitrary", "arbitrary"),
        name="mla_kv",
    )(ckv, w_ukv, side_w)


def _flash_kernel(q_ref, kn_ref, kr_ref, vv_ref, o_ref, k_ref, v_ref, sa_ref, sb_ref, m_ref, acc_ref):
    n_heads, seq, _ = q_ref.shape
    tile = ATTN_TILE
    nq = seq // tile
    tiles = [(qi, ki) for qi in range(nq) for ki in range(qi + 1)]
    s_bufs = (sa_ref, sb_ref)

    def assemble(ki):
        for g in range(n_heads):
            k_ref[g, rows(ki), :MLA_NOPE] = kn_ref[g, rows(ki), :]
            k_ref[g, rows(ki), MLA_NOPE:] = kr_ref[rows(ki), :]
            v_ref[g, rows(ki), :MLA_V] = vv_ref[g, rows(ki), :]
            v_ref[g, rows(ki), MLA_V:] = jnp.ones((tile, v_ref.shape[2] - MLA_V), v_ref.dtype)

    def rows(i):
        return slice(i * tile, (i + 1) * tile)

    def scores_into(s_ref, qi, ki):
        for g in range(n_heads):
            s_ref[g] = lax.dot_general(q_ref[g, rows(qi), :], k_ref[g, rows(ki), :],
                                       (((1,), (1,)), ((), ())), preferred_element_type=F32)

    def consume(s_ref, qi, ki):
        slot = qi % 2
        for g in range(n_heads):
            s = s_ref[g]
            if ki == qi:
                row = lax.broadcasted_iota(jnp.int32, s.shape, 0)
                col = lax.broadcasted_iota(jnp.int32, s.shape, 1)
                s = jnp.where(col <= row, s, NEG_BIG)
            m_cur = jnp.max(s, axis=-1, keepdims=True)
            if ki == 0:
                m_new = m_cur
            else:
                m_prev = m_ref[slot, g]
                m_new = jnp.maximum(m_prev, m_cur)
                alpha = jnp.exp2(m_prev - m_new)
            p = jnp.exp2(s - m_new).astype(BF16)
            pv = _bdot(p, v_ref[g, rows(ki), :])
            acc = pv if ki == 0 else alpha * acc_ref[slot, g] + pv
            if ki == qi:
                o_ref[rows(qi), g * MLA_V:(g + 1) * MLA_V] = (acc[:, :MLA_V] / acc[:, MLA_V:]).astype(o_ref.dtype)
            else:
                acc_ref[slot, g] = acc
                m_ref[slot, g] = m_new

    def step(n):
        qi, ki = tiles[n]
        if n + 1 < len(tiles):
            next_qi, next_ki = tiles[n + 1]
            if next_ki == next_qi:
                assemble(next_ki)
            scores_into(s_bufs[(n + 1) % 2], next_qi, next_ki)
        consume(s_bufs[n % 2], qi, ki)

    def steps(first):
        for n in range(first, min(first + ATTN_STEPS_PER_REGION, len(tiles))):
            step(n)

    one_trip = jnp.minimum(pl.program_id(0) + 1, 1)
    assemble(0)
    scores_into(s_bufs[0], *tiles[0])
    for first in range(0, len(tiles), ATTN_STEPS_PER_REGION):
        lax.fori_loop(0, one_trip, lambda _, carry, first=first: (steps(first), carry)[1], 0)


def _flash_attention(q, k_nope, k_rope, v, *, batch, seq):
    heads, t, _ = q.shape
    tile = ATTN_TILE
    hp = ATTN_HEADS_PER_STEP
    narrow_block = pl.BlockSpec((hp, seq, MLA_V), lambda b, h: (h, b, 0))
    return pl.pallas_call(
        _flash_kernel,
        out_shape=jax.ShapeDtypeStruct((t, heads * MLA_V), BF16),
        grid=(batch, heads // hp),
        in_specs=[pl.BlockSpec((hp, seq, MLA_PAD), lambda b, h: (h, b, 0)),
                  narrow_block,
                  pl.BlockSpec((seq, LANES), lambda b, h: (b, 0)),
                  narrow_block],
        out_specs=pl.BlockSpec((seq, hp * MLA_V), lambda b, h: (b, h)),
        scratch_shapes=[pltpu.VMEM((hp, seq, MLA_PAD), BF16),
                        pltpu.VMEM((hp, seq, 2 * MLA_V), BF16),
                        pltpu.VMEM((hp, tile, tile), F32),
                        pltpu.VMEM((hp, tile, tile), F32),
                        pltpu.VMEM((2, hp, tile, 1), F32),
                        pltpu.VMEM((2, hp, tile, 2 * MLA_V), F32)],
        compiler_params=_params("parallel", "parallel"),
        name="mla_flash_attention",
    )(q, k_nope, k_rope, v)


def _retention_layer(x, h, g_post, g_next, w_in, gn_g, w_out, layer, cos, sin, log_gamma, *, batch, seq):
    d = x.shape[1]
    dk, dv = RET_QK_DIM, RET_V_DIM
    heads = d // dk
    n_q = heads * dk
    n_v = heads * dv
    tn = PROJ_COL_TILE
    q = _proj_rope(h, w_in, layer, cos, sin, col0=0, n_cols=n_q, tn=tn, head_dim=dk,
                   scale=None, out_dtype=BF16, name="ret_q_proj")
    k = _proj_rope(h, w_in, layer, cos, sin, col0=n_q, n_cols=n_q, tn=tn, head_dim=dk,
                   scale=dk ** -0.5, out_dtype=F32, name="ret_k_proj")
    v, w_out_b = _proj(h, w_in, layer, col0=2 * n_q, n_cols=n_v, tn=tn, out_dtype=BF16,
                       name="ret_v_proj", side_w=w_out)
    y = _retention_core(q, k, v, h, w_in, layer, 2 * n_q + n_v, gn_g, log_gamma, batch=batch, seq=seq)
    return _out_proj(y, w_out_b, x, g_post, g_next, n_k=1, name="ret_out_proj")


def _mla_layer(x, h, g_post, g_next, w_in, g_q, g_kv, w_uq, w_ukv, w_out, layer, tcos, tsin, *, batch, seq):
    cq, ckv, k_rope = _mla_in(h, w_in, layer, g_q, g_kv, tcos, tsin)
    q = _mla_q(cq, _mla_q_weights(w_uq, layer), tcos, tsin)
    k_nope, v, w_out_b = _mla_kv(ckv, w_ukv, layer, w_out)
    o = _flash_attention(q, k_nope, k_rope, v, batch=batch, seq=seq)
    return _out_proj(o, w_out_b, x, g_post, g_next, n_k=1, name="mla_out_proj")


def _ffn(x, h, g_post, g_next, w_gu, w_down, layer):
    act, w_down_b = _proj_swiglu(h, w_gu, layer, w_down, tn=FFN_COL_TILE, name="ffn_gate_up")
    return _out_proj(act, w_down_b, x, g_post, g_next, n_k=FFN_DOWN_K_PARTS, name="ffn_down_proj")


def kernel(x, positions, norm_mix_pre, norm_mix_post, norm_ffn_pre, norm_ffn_post, ret_w_in, ret_gn_g, ret_w_out, mla_w_in, mla_g_q, mla_g_kv, mla_w_uq, mla_w_ukv, mla_w_out, ffn_w_gu, ffn_w_down):
    batch, seq, d = x.shape
    depth = norm_mix_pre.shape[0]
    t = batch * seq
    xf = x.reshape(t, d)
    pos_col = positions.reshape(t, 1)
    cos_r, sin_r = _rope_tables(pos_col, RET_QK_DIM)
    tcos_m, tsin_m = _mla_rope_tables(pos_col)
    ret_heads = d // RET_QK_DIM
    log_gamma = jnp.log1p(-jnp.exp2(-5.0 - jnp.arange(ret_heads, dtype=F32)))

    def row(a, i):
        return a[i].reshape(1, -1)

    h = _prenorm(xf, row(norm_mix_pre, 0))
    for i in range(depth):
        j = i // N_MIXERS
        g_ffn_pre = row(norm_ffn_pre, i)
        g_next_mix = row(norm_mix_pre, i + 1) if i + 1 < depth else None
        if i % N_MIXERS == 0:
            xf, h = _retention_layer(xf, h, row(norm_mix_post, i), g_ffn_pre, ret_w_in, row(ret_gn_g, j),
                                     ret_w_out, j, cos_r, sin_r, log_gamma, batch=batch, seq=seq)
        else:
            xf, h = _mla_layer(xf, h, row(norm_mix_post, i), g_ffn_pre, mla_w_in, row(mla_g_q, j),
                               row(mla_g_kv, j), mla_w_uq, mla_w_ukv, mla_w_out, j, tcos_m, tsin_m,
                               batch=batch, seq=seq)
        xf, h = _ffn(xf, h, row(norm_ffn_post, i), g_next_mix, ffn_w_gu, ffn_w_down, i)
    return xf.reshape(batch, seq, d)
```

```python
import functools

import jax
import jax.numpy as jnp
from jax import lax
from jax.experimental import pallas as pl
from jax.experimental.pallas import tpu as pltpu

F32 = jnp.float32
BF16 = jnp.bfloat16

RET_QK_DIM = 256
RET_V_DIM = 2 * RET_QK_DIM
MLA_NOPE = 128
MLA_ROPE = 64
MLA_V = 128
MLA_QD = MLA_NOPE + MLA_ROPE
MLA_PAD = 256
LANES = 128
ROPE_THETA = 10000.0
NORM_EPS = 1e-6
LOG2_E = 1.4426950408889634
N_MIXERS = 2

V7X_VMEM_BYTES = 64 * 1024 * 1024
VMEM_LIMIT_BYTES = V7X_VMEM_BYTES - 8 * 1024 * 1024

ROW_TILE = 1024
FFN_ROW_TILE = 2048
ROW_PARTS = 2
PROJ_COL_TILE = 1024
FFN_COL_TILE = 512
OUT_ROW_TILE = 512
FFN_DOWN_K_PARTS = 2
RET_CHUNK = 256
RET_SEQ_BLOCK = 1024
RET_HEADS_PER_STEP = 2
MLA_HEADS_PER_STEP = 8
MLA_Q_HEADS_PER_STEP = 8
ATTN_TILE = 512
ATTN_HEADS_PER_STEP = 2
ATTN_STEPS_PER_REGION = 4
OUT_COL_CHUNK = 512
NEG_BIG = -1e30


def _params(*sem):
    return pltpu.CompilerParams(dimension_semantics=sem, vmem_limit_bytes=VMEM_LIMIT_BYTES)


def _rms(x, g):
    return x * lax.rsqrt(jnp.mean(x * x, axis=-1, keepdims=True) + NORM_EPS) * g


def _bdot(a, b):
    return jnp.dot(a, b, preferred_element_type=F32)


def _silu(x):
    return x * jax.nn.sigmoid(x)


def _rope_table_kernel(pos_ref, inv_ref, cos_ref, sin_ref):
    ang = pos_ref[...].astype(F32) * inv_ref[...]
    cos_ref[...] = jnp.cos(ang)
    sin_ref[...] = jnp.sin(ang)


def _rope_tables(pos_col, d):
    t = pos_col.shape[0]
    f = d // 2
    inv = (ROPE_THETA ** (-jnp.arange(0, d, 2, dtype=F32) / d)).reshape(1, f)
    tm = ROW_TILE
    return pl.pallas_call(
        _rope_table_kernel,
        out_shape=(jax.ShapeDtypeStruct((t, f), F32), jax.ShapeDtypeStruct((t, f), F32)),
        grid=(t // tm,),
        in_specs=[pl.BlockSpec((tm, 1), lambda i: (i, 0)),
                  pl.BlockSpec((1, f), lambda i: (0, 0))],
        out_specs=(pl.BlockSpec((tm, f), lambda i: (i, 0)),
                   pl.BlockSpec((tm, f), lambda i: (i, 0))),
        compiler_params=_params("parallel"),
        name=f"rope_tables_{d}",
    )(pos_col, inv)


def _prenorm_kernel(x_ref, g_ref, h_ref):
    h_ref[...] = _rms(x_ref[...], g_ref[...]).astype(h_ref.dtype)


def _prenorm(x, g):
    t, d = x.shape
    tm = ROW_TILE // 2
    return pl.pallas_call(
        _prenorm_kernel,
        out_shape=jax.ShapeDtypeStruct((t, d), BF16),
        grid=(t // tm,),
        in_specs=[pl.BlockSpec((tm, d), lambda i: (i, 0)),
                  pl.BlockSpec((1, d), lambda i: (0, 0))],
        out_specs=pl.BlockSpec((tm, d), lambda i: (i, 0)),
        compiler_params=_params("parallel"),
        name="prenorm",
    )(x, g)


def _row_parts(n_rows):
    part = n_rows // ROW_PARTS
    return [slice(r * part, (r + 1) * part) for r in range(ROW_PARTS)]


def _cast_weight_once(w_ref, wb_ref):
    @pl.when(pl.program_id(1) == 0)
    def _():
        wb_ref[...] = w_ref[...].astype(BF16)


def _proj_plain_kernel(h_ref, w_ref, side_ref, o_ref, side_o_ref, wb_ref):
    side_o_ref[...] = side_ref[...].astype(side_o_ref.dtype)
    _cast_weight_once(w_ref, wb_ref)
    for rows in _row_parts(h_ref.shape[0]):
        o_ref[rows, :] = _bdot(h_ref[rows, :], wb_ref[...]).astype(o_ref.dtype)


def _proj_rope_kernel(h_ref, w_ref, cos_ref, sin_ref, o_ref, wb_ref, *, head_dim, scale):
    _cast_weight_once(w_ref, wb_ref)
    half = head_dim // 2
    for rows in _row_parts(h_ref.shape[0]):
        acc = _bdot(h_ref[rows, :], wb_ref[...])
        cos, sin = cos_ref[rows, :], sin_ref[rows, :]
        for c0 in range(0, acc.shape[1], head_dim):
            x1, x2 = acc[:, c0:c0 + half], acc[:, c0 + half:c0 + head_dim]
            r1 = x1 * cos - x2 * sin
            r2 = x1 * sin + x2 * cos
            if scale is not None:
                r1, r2 = r1 * scale, r2 * scale
            o_ref[rows, c0:c0 + half] = r1.astype(o_ref.dtype)
            o_ref[rows, c0 + half:c0 + head_dim] = r2.astype(o_ref.dtype)


def _proj_swiglu_kernel(h_ref, wg_ref, wu_ref, side_ref, o_ref, side_o_ref, wgb_ref, wub_ref):
    side_o_ref[...] = side_ref[...].astype(side_o_ref.dtype)
    _cast_weight_once(wg_ref, wgb_ref)
    _cast_weight_once(wu_ref, wub_ref)
    for rows in _row_parts(h_ref.shape[0]):
        h = h_ref[rows, :]
        gate = _bdot(h, wgb_ref[...])
        up = _bdot(h, wub_ref[...])
        o_ref[rows, :] = (_silu(gate) * up).astype(o_ref.dtype)


def _side_cast_specs(side_w, layer, n_steps, n_inner):
    _, kdim, n = side_w.shape
    rows = kdim // n_steps
    assert rows * n_steps == kdim and rows % 16 == 0, (kdim, n_steps)
    in_spec = pl.BlockSpec((None, rows, n), lambda j, i: (layer, j * n_inner + i, 0))
    out_spec = pl.BlockSpec((rows, n), lambda j, i: (j * n_inner + i, 0))
    return in_spec, out_spec, jax.ShapeDtypeStruct((kdim, n), BF16)


def _proj(h, w, layer, side_w, *, col0, n_cols, tn, out_dtype, name):
    t, kdim = h.shape
    tm = ROW_TILE
    tn = min(tn, n_cols)
    off = col0 // tn
    grid = (n_cols // tn, t // tm)
    side_in, side_out, side_shape = _side_cast_specs(side_w, layer, grid[0] * grid[1], grid[1])
    return pl.pallas_call(
        _proj_plain_kernel,
        out_shape=(jax.ShapeDtypeStruct((t, n_cols), out_dtype), side_shape),
        grid=grid,
        in_specs=[pl.BlockSpec((tm, kdim), lambda j, i: (i, 0)),
                  pl.BlockSpec((None, kdim, tn), lambda j, i: (layer, 0, j + off)),
                  side_in],
        out_specs=(pl.BlockSpec((tm, tn), lambda j, i: (i, j)), side_out),
        scratch_shapes=[pltpu.VMEM((kdim, tn), BF16)],
        compiler_params=_params("arbitrary", "arbitrary"),
        name=name,
    )(h, w, side_w)


def _proj_rope(h, w, layer, cos, sin, *, col0, n_cols, tn, head_dim, scale, out_dtype, name):
    t, kdim = h.shape
    tm = ROW_TILE
    tn = min(tn, n_cols)
    off = col0 // tn
    kern = functools.partial(_proj_rope_kernel, head_dim=head_dim, scale=scale)
    return pl.pallas_call(
        kern,
        out_shape=jax.ShapeDtypeStruct((t, n_cols), out_dtype),
        grid=(n_cols // tn, t // tm),
        in_specs=[pl.BlockSpec((tm, kdim), lambda j, i: (i, 0)),
                  pl.BlockSpec((None, kdim, tn), lambda j, i: (layer, 0, j + off)),
                  pl.BlockSpec((tm, head_dim // 2), lambda j, i: (i, 0)),
                  pl.BlockSpec((tm, head_dim // 2), lambda j, i: (i, 0))],
        out_specs=pl.BlockSpec((tm, tn), lambda j, i: (i, j)),
        scratch_shapes=[pltpu.VMEM((kdim, tn), BF16)],
        compiler_params=_params("arbitrary", "arbitrary"),
        name=name,
    )(h, w, cos, sin)


def _proj_swiglu(h, w_gu, layer, side_w, *, tn, name):
    t, kdim = h.shape
    d_ff = w_gu.shape[2] // 2
    tm = FFN_ROW_TILE
    n_tiles = d_ff // tn
    grid = (n_tiles, t // tm)
    side_in, side_out, side_shape = _side_cast_specs(side_w, layer, grid[0] * grid[1], grid[1])
    return pl.pallas_call(
        _proj_swiglu_kernel,
        out_shape=(jax.ShapeDtypeStruct((t, d_ff), BF16), side_shape),
        grid=grid,
        in_specs=[pl.BlockSpec((tm, kdim), lambda j, i: (i, 0)),
                  pl.BlockSpec((None, kdim, tn), lambda j, i: (layer, 0, j)),
                  pl.BlockSpec((None, kdim, tn), lambda j, i: (layer, 0, j + n_tiles)),
                  side_in],
        out_specs=(pl.BlockSpec((tm, tn), lambda j, i: (i, j)), side_out),
        scratch_shapes=[pltpu.VMEM((kdim, tn), BF16), pltpu.VMEM((kdim, tn), BF16)],
        compiler_params=_params("arbitrary", "arbitrary"),
        name=name,
    )(h, w_gu, w_gu, side_w)


def _out_proj_kernel(a_ref, w_ref, x_ref, g_ref, *rest, n_k, emit_next):
    if emit_next:
        gn_ref, o_ref, h_ref = rest
    else:
        (o_ref,) = rest
    k = pl.program_id(1)
    tk = a_ref.shape[1]

    def accumulate(kk, rows):
        a = a_ref[rows, :]
        for n in range(0, o_ref.shape[1], OUT_COL_CHUNK):
            cols = slice(n, n + OUT_COL_CHUNK)
            part = _bdot(a, w_ref[kk * tk:(kk + 1) * tk, cols])
            if kk == 0:
                o_ref[rows, cols] = part
            else:
                o_ref[rows, cols] += part

    def epilogue(rows):
        x_new = x_ref[rows, :] + _rms(o_ref[rows, :], g_ref[...])
        o_ref[rows, :] = x_new
        if emit_next:
            h_ref[rows, :] = _rms(x_new, gn_ref[...]).astype(h_ref.dtype)

    for kk in range(n_k):
        @pl.when(k == kk)
        def _(kk=kk):
            for rows in _row_parts(a_ref.shape[0]):
                accumulate(kk, rows)
                if kk == n_k - 1:
                    epilogue(rows)


def _out_proj(a, w_bf16, x, g_post, g_next, *, n_k, name):
    t, kdim = a.shape
    d = w_bf16.shape[1]
    tm = OUT_ROW_TILE
    tk = kdim // n_k
    emit_next = g_next is not None
    kern = functools.partial(_out_proj_kernel, n_k=n_k, emit_next=emit_next)
    row_spec = pl.BlockSpec((tm, d), lambda i, k: (i, 0))
    gain_spec = pl.BlockSpec((1, d), lambda i, k: (0, 0))
    operands = [a, w_bf16, x, g_post]
    in_specs = [pl.BlockSpec((tm, tk), lambda i, k: (i, k)),
                pl.BlockSpec((kdim, d), lambda i, k: (0, 0), pipeline_mode=pl.Buffered(1)),
                row_spec, gain_spec]
    out_shape = [jax.ShapeDtypeStruct((t, d), F32)]
    out_specs = [row_spec]
    if emit_next:
        operands.append(g_next)
        in_specs.append(gain_spec)
        out_shape.append(jax.ShapeDtypeStruct((t, d), BF16))
        out_specs.append(row_spec)
    res = pl.pallas_call(
        kern,
        out_shape=tuple(out_shape),
        grid=(t // tm, n_k),
        in_specs=in_specs,
        out_specs=tuple(out_specs),
        compiler_params=_params("arbitrary", "arbitrary"),
        name=name,
    )(*operands)
    return (res[0], res[1]) if emit_next else (res[0], None)


def _retention_kernel(lg_ref, q_ref, k_ref, v_ref, h_ref, wg_ref, gn_ref, o_ref,
                      state_ref, wgb_ref, gate_ref, *, heads_per_step):
    c_len = RET_CHUNK
    dk, dv = RET_QK_DIM, RET_V_DIM
    n_chunks = q_ref.shape[0] // c_len
    first_block_of_heads = jnp.logical_and(pl.program_id(1) == 0, pl.program_id(2) == 0)

    @pl.when(first_block_of_heads)
    def _():
        wgb_ref[...] = wg_ref[...].astype(BF16)

    @pl.when(pl.program_id(2) == 0)
    def _():
        state_ref[...] = jnp.zeros_like(state_ref)

    row = lax.broadcasted_iota(jnp.int32, (c_len, c_len), 0)
    col = lax.broadcasted_iota(jnp.int32, (c_len, c_len), 1)
    rel = (row - col).astype(F32)
    causal = rel >= 0
    rel_pos = jnp.where(causal, rel, 0.0)
    idx = lax.broadcasted_iota(jnp.int32, (c_len, 1), 0).astype(F32)
    decays = []
    for g in range(heads_per_step):
        lg = jnp.full((1, 1), lg_ref[pl.program_id(0) * heads_per_step + g], F32)
        decays.append((jnp.where(causal, jnp.exp(lg * rel_pos), 0.0),
                       jnp.exp(lg * (idx + 1.0)),
                       jnp.exp(lg * (c_len - 1.0 - idx)),
                       jnp.exp(lg * c_len)))

    for rows in _row_parts(h_ref.shape[0]):
        for g in range(heads_per_step):
            v_cols = slice(g * dv, (g + 1) * dv)
            gate_ref[rows, v_cols] = _silu(_bdot(h_ref[rows, :], wgb_ref[:, v_cols])).astype(gate_ref.dtype)

    for c in range(n_chunks):
        rows = slice(c * c_len, (c + 1) * c_len)
        for g in range(heads_per_step):
            decay_in, decay_q, decay_k, decay_chunk = decays[g]
            qk_cols = slice(g * dk, (g + 1) * dk)
            v_cols = slice(g * dv, (g + 1) * dv)
            q = q_ref[rows, qk_cols]
            k = k_ref[rows, qk_cols]
            v = v_ref[rows, v_cols]
            scores = lax.dot_general(q, k.astype(BF16), (((1,), (1,)), ((), ())),
                                     preferred_element_type=F32) * decay_in
            state = state_ref[g]
            y = _bdot(scores.astype(BF16), v) + _bdot(q, state.astype(BF16)) * decay_q
            kd_t = (k * decay_k).T.astype(BF16)
            state_ref[g] = state * decay_chunk + _bdot(kd_t, v)

            mu = jnp.mean(y, axis=-1, keepdims=True)
            yc = y - mu
            var = jnp.mean(yc * yc, axis=-1, keepdims=True)
            yn = yc * lax.rsqrt(var + NORM_EPS) * gn_ref[:, v_cols]
            o_ref[rows, v_cols] = (gate_ref[rows, v_cols] * yn).astype(o_ref.dtype)


def _retention_core(q, k, v, h, w_in, layer, gate_col0, gn_g, log_gamma, *, batch, seq):
    t, d = h.shape
    dk, dv = RET_QK_DIM, RET_V_DIM
    heads = v.shape[1] // dv
    hp = RET_HEADS_PER_STEP
    lb = RET_SEQ_BLOCK
    nb = seq // lb
    gate_off = gate_col0 // (hp * dv)
    kern = functools.partial(_retention_kernel, heads_per_step=hp)
    return pl.pallas_call(
        kern,
        out_shape=jax.ShapeDtypeStruct((t, heads * dv), BF16),
        grid=(heads // hp, batch, nb),
        in_specs=[pl.BlockSpec(memory_space=pltpu.SMEM),
                  pl.BlockSpec((lb, hp * dk), lambda h_, b, s: (b * nb + s, h_)),
                  pl.BlockSpec((lb, hp * dk), lambda h_, b, s: (b * nb + s, h_)),
                  pl.BlockSpec((lb, hp * dv), lambda h_, b, s: (b * nb + s, h_)),
                  pl.BlockSpec((lb, d), lambda h_, b, s: (b * nb + s, 0)),
                  pl.BlockSpec((None, d, hp * dv), lambda h_, b, s: (layer, 0, gate_off + h_)),
                  pl.BlockSpec((1, hp * dv), lambda h_, b, s: (0, h_))],
        out_specs=pl.BlockSpec((lb, hp * dv), lambda h_, b, s: (b * nb + s, h_)),
        scratch_shapes=[pltpu.VMEM((hp, dk, dv), F32),
                        pltpu.VMEM((d, hp * dv), BF16),
                        pltpu.VMEM((lb, hp * dv), BF16)],
        compiler_params=_params("arbitrary", "arbitrary", "arbitrary"),
        name="retention_core",
    )(log_gamma, q, k, v, h, w_in, gn_g)


def _mla_rope_table_kernel(pos_ref, inv_ref, cos_mask_ref, sin_sign_ref, tcos_ref, tsin_ref):
    ang = pos_ref[...].astype(F32) * inv_ref[...]
    tcos_ref[...] = jnp.cos(ang) * cos_mask_ref[...]
    tsin_ref[...] = jnp.sin(ang) * sin_sign_ref[...]


def _mla_rope_tables(pos_col):
    t = pos_col.shape[0]
    half = MLA_ROPE // 2
    inv = ROPE_THETA ** (-jnp.arange(0, MLA_ROPE, 2, dtype=F32) / MLA_ROPE)
    ones, zeros = jnp.ones((half,), F32), jnp.zeros((half,), F32)
    inv_row = jnp.concatenate([inv, inv, zeros, zeros]).reshape(1, LANES)
    cos_mask = jnp.concatenate([ones, ones, zeros, zeros]).reshape(1, LANES)
    sin_sign = jnp.concatenate([-ones, ones, zeros, zeros]).reshape(1, LANES)
    tm = ROW_TILE
    row = pl.BlockSpec((1, LANES), lambda i: (0, 0))
    return pl.pallas_call(
        _mla_rope_table_kernel,
        out_shape=(jax.ShapeDtypeStruct((t, LANES), F32), jax.ShapeDtypeStruct((t, LANES), F32)),
        grid=(t // tm,),
        in_specs=[pl.BlockSpec((tm, 1), lambda i: (i, 0)), row, row, row],
        out_specs=(pl.BlockSpec((tm, LANES), lambda i: (i, 0)),
                   pl.BlockSpec((tm, LANES), lambda i: (i, 0))),
        compiler_params=_params("parallel"),
        name="mla_rope_tables",
    )(pos_col, inv_row, cos_mask, sin_sign)


def _mla_in_kernel(h_ref, w_ref, gq_ref, gkv_ref, tcos_ref, tsin_ref,
                   cq_ref, ckv_ref, kr_ref, wb_ref, *, q_lora, kv_lora):
    @pl.when(pl.program_id(0) == 0)
    def _():
        wb_ref[...] = w_ref[...].astype(BF16)

    for rows in _row_parts(h_ref.shape[0]):
        c = _bdot(h_ref[rows, :], wb_ref[...])
        cq_ref[rows, :] = _rms(c[:, :q_lora], gq_ref[...]).astype(cq_ref.dtype)
        ckv_ref[rows, :] = _rms(c[:, q_lora:q_lora + kv_lora], gkv_ref[...]).astype(ckv_ref.dtype)
        kr = c[:, q_lora + kv_lora:]
        half = kr.shape[1] // 2
        v = jnp.concatenate([kr, jnp.zeros_like(kr)], axis=-1)
        lane = lax.broadcasted_iota(jnp.int32, v.shape, 1)
        swapped = jnp.where(lane < half, pltpu.roll(v, LANES - half, 1), pltpu.roll(v, half, 1))
        swapped = jnp.where(lane < 2 * half, swapped, 0.0)
        kr_ref[rows, :] = (v * tcos_ref[rows, :] + swapped * tsin_ref[rows, :]).astype(kr_ref.dtype)


def _mla_in(h, w, layer, gq, gkv, tcos, tsin):
    t, d = h.shape
    q_lora, kv_lora = gq.shape[1], gkv.shape[1]
    n_in = w.shape[2]
    tm = ROW_TILE
    kern = functools.partial(_mla_in_kernel, q_lora=q_lora, kv_lora=kv_lora)
    return pl.pallas_call(
        kern,
        out_shape=(jax.ShapeDtypeStruct((t, q_lora), BF16),
                   jax.ShapeDtypeStruct((t, kv_lora), BF16),
                   jax.ShapeDtypeStruct((t, LANES), BF16)),
        grid=(t // tm,),
        in_specs=[pl.BlockSpec((tm, d), lambda i: (i, 0)),
                  pl.BlockSpec((None, d, n_in), lambda i: (layer, 0, 0)),
                  pl.BlockSpec((1, q_lora), lambda i: (0, 0)),
                  pl.BlockSpec((1, kv_lora), lambda i: (0, 0)),
                  pl.BlockSpec((tm, LANES), lambda i: (i, 0)),
                  pl.BlockSpec((tm, LANES), lambda i: (i, 0))],
        out_specs=(pl.BlockSpec((tm, q_lora), lambda i: (i, 0)),
                   pl.BlockSpec((tm, kv_lora), lambda i: (i, 0)),
                   pl.BlockSpec((tm, LANES), lambda i: (i, 0))),
        scratch_shapes=[pltpu.VMEM((d, n_in), BF16)],
        compiler_params=_params("arbitrary"),
        name="mla_in",
    )(h, w, gq, gkv, tcos, tsin)


def _mla_q_weight_kernel(w_ref, o_ref):
    half = MLA_ROPE // 2
    w = w_ref[...]
    for t in range(o_ref.shape[0]):
        o = t * MLA_QD
        nope = w[:, o:o + MLA_NOPE]
        x1 = w[:, o + MLA_NOPE:o + MLA_NOPE + half]
        x2 = w[:, o + MLA_NOPE + half:o + MLA_QD]
        o_ref[t] = jnp.concatenate([nope, x1, x2, x2, x1], axis=-1).astype(o_ref.dtype)


def _mla_q_weights(w_uq, layer):
    q_lora = w_uq.shape[1]
    heads = w_uq.shape[2] // MLA_QD
    hp = 2
    return pl.pallas_call(
        _mla_q_weight_kernel,
        out_shape=jax.ShapeDtypeStruct((heads, q_lora, MLA_PAD), BF16),
        grid=(heads // hp,),
        in_specs=[pl.BlockSpec((None, q_lora, hp * MLA_QD), lambda h: (layer, 0, h))],
        out_specs=pl.BlockSpec((hp, q_lora, MLA_PAD), lambda h: (h, 0, 0)),
        compiler_params=_params("parallel"),
        name="mla_q_weights",
    )(w_uq)


def _mla_q_kernel(cq_ref, w_ref, tcos_ref, tsin_ref, o_ref, *, scale):
    cq = cq_ref[...]
    tcos, tsin = tcos_ref[...], tsin_ref[...]
    for g in range(o_ref.shape[0]):
        res = _bdot(cq, w_ref[g])
        rot = res[:, MLA_NOPE:]
        rope = rot * tcos + pltpu.roll(rot, LANES // 2, 1) * tsin
        o_ref[g, :, :MLA_NOPE] = (res[:, :MLA_NOPE] * scale).astype(o_ref.dtype)
        o_ref[g, :, MLA_NOPE:] = (rope * scale).astype(o_ref.dtype)


def _mla_q(cq, w_q, tcos, tsin):
    t, q_lora = cq.shape
    heads = w_q.shape[0]
    tm = ROW_TILE
    hp = min(MLA_Q_HEADS_PER_STEP, heads)
    kern = functools.partial(_mla_q_kernel, scale=MLA_QD ** -0.5 * LOG2_E)
    return pl.pallas_call(
        kern,
        out_shape=jax.ShapeDtypeStruct((heads, t, MLA_PAD), BF16),
        grid=(t // tm, heads // hp),
        in_specs=[pl.BlockSpec((tm, q_lora), lambda i, h: (i, 0)),
                  pl.BlockSpec((hp, q_lora, MLA_PAD), lambda i, h: (h, 0, 0)),
                  pl.BlockSpec((tm, LANES), lambda i, h: (i, 0)),
                  pl.BlockSpec((tm, LANES), lambda i, h: (i, 0))],
        out_specs=pl.BlockSpec((hp, tm, MLA_PAD), lambda i, h: (h, i, 0)),
        compiler_params=_params("parallel", "parallel"),
        name="mla_q",
    )(cq, w_q, tcos, tsin)


def _mla_kv_kernel(ckv_ref, w_ref, side_ref, k_ref, v_ref, side_o_ref):
    side_o_ref[...] = side_ref[...].astype(side_o_ref.dtype)
    ckv = ckv_ref[...]
    per_head = MLA_NOPE + MLA_V
    for g in range(k_ref.shape[0]):
        kv = _bdot(ckv, w_ref[:, g * per_head:(g + 1) * per_head].astype(BF16))
        k_ref[g] = kv[:, :MLA_NOPE].astype(k_ref.dtype)
        v_ref[g] = kv[:, MLA_NOPE:].astype(v_ref.dtype)


def _mla_kv(ckv, w_ukv, layer, side_w):
    t, kv_lora = ckv.shape
    per_head = MLA_NOPE + MLA_V
    heads = w_ukv.shape[2] // per_head
    tm = ROW_TILE
    hp = min(MLA_HEADS_PER_STEP, heads)
    grid = (t // tm, heads // hp)
    side_in, side_out, side_shape = _side_cast_specs(side_w, layer, grid[0] * grid[1], grid[1])
    return pl.pallas_call(
        _mla_kv_kernel,
        out_shape=(jax.ShapeDtypeStruct((heads, t, MLA_NOPE), BF16),
                   jax.ShapeDtypeStruct((heads, t, MLA_V), BF16),
                   side_shape),
        grid=grid,
        in_specs=[pl.BlockSpec((tm, kv_lora), lambda i, h: (i, 0)),
                  pl.BlockSpec((None, kv_lora, hp * per_head), lambda i, h: (layer, 0, h)),
                  side_in],
        out_specs=(pl.BlockSpec((hp, tm, MLA_NOPE), lambda i, h: (h, i, 0)),
                   pl.BlockSpec((hp, tm, MLA_V), lambda i, h: (h, i, 0)),
                   side_out),
        compiler_params=_params("arbitrary", "arbitrary"),
        name="mla_kv",
    )(ckv, w_ukv, side_w)


def _flash_kernel(q_ref, kn_ref, kr_ref, vv_ref, o_ref, k_ref, v_ref, sa_ref, sb_ref, m_ref, acc_ref):
    n_heads, seq, _ = q_ref.shape
    tile = ATTN_TILE
    nq = seq // tile
    tiles = [(qi, ki) for qi in range(nq) for ki in range(qi + 1)]
    s_bufs = (sa_ref, sb_ref)

    def assemble(ki):
        for g in range(n_heads):
            k_ref[g, rows(ki), :MLA_NOPE] = kn_ref[g, rows(ki), :]
            k_ref[g, rows(ki), MLA_NOPE:] = kr_ref[rows(ki), :]
            v_ref[g, rows(ki), :MLA_V] = vv_ref[g, rows(ki), :]
            v_ref[g, rows(ki), MLA_V:] = jnp.ones((tile, v_ref.shape[2] - MLA_V), v_ref.dtype)

    def rows(i):
        return slice(i * tile, (i + 1) * tile)

    def row_parts(qi, ki):
        if ki != qi:
            return [(0, tile, tile)]
        return [(0, tile // 2, tile // 2), (tile // 2, tile, tile)]

    def scores_into(s_ref, qi, ki):
        for g in range(n_heads):
            for r0, r1, n_kv in row_parts(qi, ki):
                q = q_ref[g, qi * tile + r0:qi * tile + r1, :]
                k = k_ref[g, ki * tile:ki * tile + n_kv, :]
                s_ref[g, r0:r1, :n_kv] = lax.dot_general(q, k, (((1,), (1,)), ((), ())),
                                                         preferred_element_type=F32)

    def consume(s_ref, qi, ki):
        slot = qi % 2
        for g in range(n_heads):
            for r0, r1, n_kv in row_parts(qi, ki):
                s = s_ref[g, r0:r1, :n_kv]
                if ki == qi:
                    row = lax.broadcasted_iota(jnp.int32, s.shape, 0) + r0
                    col = lax.broadcasted_iota(jnp.int32, s.shape, 1)
                    s = jnp.where(col <= row, s, NEG_BIG)
                m_cur = jnp.max(s, axis=-1, keepdims=True)
                if ki == 0:
                    m_new = m_cur
                else:
                    m_prev = m_ref[slot, g, r0:r1]
                    m_new = jnp.maximum(m_prev, m_cur)
                    alpha = jnp.exp2(m_prev - m_new)
                p = jnp.exp2(s - m_new).astype(BF16)
                pv = _bdot(p, v_ref[g, ki * tile:ki * tile + n_kv, :])
                acc = pv if ki == 0 else alpha * acc_ref[slot, g, r0:r1] + pv
                if ki == qi:
                    out = (acc[:, :MLA_V] / acc[:, MLA_V:]).astype(o_ref.dtype)
                    o_ref[qi * tile + r0:qi * tile + r1, g * MLA_V:(g + 1) * MLA_V] = out
                else:
                    acc_ref[slot, g, r0:r1] = acc
                    m_ref[slot, g, r0:r1] = m_new

    def step(n):
        qi, ki = tiles[n]
        if n + 1 < len(tiles):
            next_qi, next_ki = tiles[n + 1]
            if next_ki == next_qi:
                assemble(next_ki)
            scores_into(s_bufs[(n + 1) % 2], next_qi, next_ki)
        consume(s_bufs[n % 2], qi, ki)

    def steps(first):
        for n in range(first, min(first + ATTN_STEPS_PER_REGION, len(tiles))):
            step(n)

    one_trip = jnp.minimum(pl.program_id(0) + 1, 1)
    assemble(0)
    scores_into(s_bufs[0], *tiles[0])
    for first in range(0, len(tiles), ATTN_STEPS_PER_REGION):
        lax.fori_loop(0, one_trip, lambda _, carry, first=first: (steps(first), carry)[1], 0)


def _flash_attention(q, k_nope, k_rope, v, *, batch, seq):
    heads, t, _ = q.shape
    tile = ATTN_TILE
    hp = ATTN_HEADS_PER_STEP
    narrow_block = pl.BlockSpec((hp, seq, MLA_V), lambda b, h: (h, b, 0))
    return pl.pallas_call(
        _flash_kernel,
        out_shape=jax.ShapeDtypeStruct((t, heads * MLA_V), BF16),
        grid=(batch, heads // hp),
        in_specs=[pl.BlockSpec((hp, seq, MLA_PAD), lambda b, h: (h, b, 0)),
                  narrow_block,
                  pl.BlockSpec((seq, LANES), lambda b, h: (b, 0)),
                  narrow_block],
        out_specs=pl.BlockSpec((seq, hp * MLA_V), lambda b, h: (b, h)),
        scratch_shapes=[pltpu.VMEM((hp, seq, MLA_PAD), BF16),
                        pltpu.VMEM((hp, seq, 2 * MLA_V), BF16),
                        pltpu.VMEM((hp, tile, tile), F32),
                        pltpu.VMEM((hp, tile, tile), F32),
                        pltpu.VMEM((2, hp, tile, 1), F32),
                        pltpu.VMEM((2, hp, tile, 2 * MLA_V), F32)],
        compiler_params=_params("parallel", "parallel"),
        name="mla_flash_attention",
    )(q, k_nope, k_rope, v)


def _retention_layer(x, h, g_post, g_next, w_in, gn_g, w_out, layer, cos, sin, log_gamma, *, batch, seq):
    d = x.shape[1]
    dk, dv = RET_QK_DIM, RET_V_DIM
    heads = d // dk
    n_q = heads * dk
    n_v = heads * dv
    tn = PROJ_COL_TILE
    q = _proj_rope(h, w_in, layer, cos, sin, col0=0, n_cols=n_q, tn=tn, head_dim=dk,
                   scale=None, out_dtype=BF16, name="ret_q_proj")
    k = _proj_rope(h, w_in, layer, cos, sin, col0=n_q, n_cols=n_q, tn=tn, head_dim=dk,
                   scale=dk ** -0.5, out_dtype=F32, name="ret_k_proj")
    v, w_out_b = _proj(h, w_in, layer, w_out, col0=2 * n_q, n_cols=n_v, tn=tn, out_dtype=BF16,
                       name="ret_v_proj")
    y = _retention_core(q, k, v, h, w_in, layer, 2 * n_q + n_v, gn_g, log_gamma, batch=batch, seq=seq)
    return _out_proj(y, w_out_b, x, g_post, g_next, n_k=1, name="ret_out_proj")


def _mla_layer(x, h, g_post, g_next, w_in, g_q, g_kv, w_uq, w_ukv, w_out, layer, tcos, tsin, *, batch, seq):
    cq, ckv, k_rope = _mla_in(h, w_in, layer, g_q, g_kv, tcos, tsin)
    q = _mla_q(cq, _mla_q_weights(w_uq, layer), tcos, tsin)
    k_nope, v, w_out_b = _mla_kv(ckv, w_ukv, layer, w_out)
    o = _flash_attention(q, k_nope, k_rope, v, batch=batch, seq=seq)
    return _out_proj(o, w_out_b, x, g_post, g_next, n_k=1, name="mla_out_proj")


def _ffn(x, h, g_post, g_next, w_gu, w_down, layer):
    act, w_down_b = _proj_swiglu(h, w_gu, layer, w_down, tn=FFN_COL_TILE, name="ffn_gate_up")
    return _out_proj(act, w_down_b, x, g_post, g_next, n_k=FFN_DOWN_K_PARTS, name="ffn_down_proj")


def kernel(x, positions, norm_mix_pre, norm_mix_post, norm_ffn_pre, norm_ffn_post, ret_w_in, ret_gn_g, ret_w_out, mla_w_in, mla_g_q, mla_g_kv, mla_w_uq, mla_w_ukv, mla_w_out, ffn_w_gu, ffn_w_down):
    batch, seq, d = x.shape
    depth = norm_mix_pre.shape[0]
    t = batch * seq
    xf = x.reshape(t, d)
    pos_col = positions.reshape(t, 1)
    cos_r, sin_r = _rope_tables(pos_col, RET_QK_DIM)
    tcos_m, tsin_m = _mla_rope_tables(pos_col)
    ret_heads = d // RET_QK_DIM
    log_gamma = jnp.log1p(-jnp.exp2(-5.0 - jnp.arange(ret_heads, dtype=F32)))

    def row(a, i):
        return a[i].reshape(1, -1)

    h = _prenorm(xf, row(norm_mix_pre, 0))
    for i in range(depth):
        j = i // N_MIXERS
        g_ffn_pre = row(norm_ffn_pre, i)
        g_next_mix = row(norm_mix_pre, i + 1) if i + 1 < depth else None
        if i % N_MIXERS == 0:
            xf, h = _retention_layer(xf, h, row(norm_mix_post, i), g_ffn_pre, ret_w_in, row(ret_gn_g, j),
                                     ret_w_out, j, cos_r, sin_r, log_gamma, batch=batch, seq=seq)
        else:
            xf, h = _mla_layer(xf, h, row(norm_mix_post, i), g_ffn_pre, mla_w_in, row(mla_g_q, j),
                               row(mla_g_kv, j), mla_w_uq, mla_w_ukv, mla_w_out, j, tcos_m, tsin_m,
                               batch=batch, seq=seq)
        xf, h = _ffn(xf, h, row(norm_ffn_post, i), g_next_mix, ffn_w_gu, ffn_w_down, i)
    return xf.reshape(batch, seq, d)
```

```python
import functools

import jax
import jax.numpy as jnp
from jax import lax
from jax.experimental import pallas as pl
from jax.experimental.pallas import tpu as pltpu

F32 = jnp.float32
BF16 = jnp.bfloat16

RET_QK_DIM = 256
RET_V_DIM = 2 * RET_QK_DIM
MLA_NOPE = 128
MLA_ROPE = 64
MLA_V = 128
MLA_QD = MLA_NOPE + MLA_ROPE
MLA_PAD = 256
LANES = 128
ROPE_THETA = 10000.0
NORM_EPS = 1e-6
LOG2_E = 1.4426950408889634
N_MIXERS = 2

V7X_VMEM_BYTES = 64 * 1024 * 1024
VMEM_LIMIT_BYTES = V7X_VMEM_BYTES - 8 * 1024 * 1024

ROW_TILE = 1024
FFN_ROW_TILE = 2048
ROW_PARTS = 2
PROJ_COL_TILE = 1024
FFN_COL_TILE = 512
OUT_ROW_TILE = 512
FFN_DOWN_K_PARTS = 2
RET_CHUNK = 256
RET_SEQ_BLOCK = 1024
RET_HEADS_PER_STEP = 2
MLA_HEADS_PER_STEP = 8
MLA_Q_HEADS_PER_STEP = 8
ATTN_TILE = 512
ATTN_HEADS_PER_STEP = 4
ATTN_STEPS_PER_REGION = 4
OUT_COL_CHUNK = 512
NEG_BIG = -1e30


def _params(*sem):
    return pltpu.CompilerParams(dimension_semantics=sem, vmem_limit_bytes=VMEM_LIMIT_BYTES)


def _rms(x, g):
    return x * lax.rsqrt(jnp.mean(x * x, axis=-1, keepdims=True) + NORM_EPS) * g


def _bdot(a, b):
    return jnp.dot(a, b, preferred_element_type=F32)


def _silu(x):
    return x * jax.nn.sigmoid(x)


def _rope_table_kernel(pos_ref, inv_ref, cos_ref, sin_ref):
    ang = pos_ref[...].astype(F32) * inv_ref[...]
    cos_ref[...] = jnp.cos(ang)
    sin_ref[...] = jnp.sin(ang)


def _rope_tables(pos_col, d):
    t = pos_col.shape[0]
    f = d // 2
    inv = (ROPE_THETA ** (-jnp.arange(0, d, 2, dtype=F32) / d)).reshape(1, f)
    tm = ROW_TILE
    return pl.pallas_call(
        _rope_table_kernel,
        out_shape=(jax.ShapeDtypeStruct((t, f), F32), jax.ShapeDtypeStruct((t, f), F32)),
        grid=(t // tm,),
        in_specs=[pl.BlockSpec((tm, 1), lambda i: (i, 0)),
                  pl.BlockSpec((1, f), lambda i: (0, 0))],
        out_specs=(pl.BlockSpec((tm, f), lambda i: (i, 0)),
                   pl.BlockSpec((tm, f), lambda i: (i, 0))),
        compiler_params=_params("parallel"),
        name=f"rope_tables_{d}",
    )(pos_col, inv)


def _prenorm_kernel(x_ref, g_ref, h_ref):
    h_ref[...] = _rms(x_ref[...], g_ref[...]).astype(h_ref.dtype)


def _prenorm(x, g):
    t, d = x.shape
    tm = ROW_TILE // 2
    return pl.pallas_call(
        _prenorm_kernel,
        out_shape=jax.ShapeDtypeStruct((t, d), BF16),
        grid=(t // tm,),
        in_specs=[pl.BlockSpec((tm, d), lambda i: (i, 0)),
                  pl.BlockSpec((1, d), lambda i: (0, 0))],
        out_specs=pl.BlockSpec((tm, d), lambda i: (i, 0)),
        compiler_params=_params("parallel"),
        name="prenorm",
    )(x, g)


def _row_parts(n_rows):
    part = n_rows // ROW_PARTS
    return [slice(r * part, (r + 1) * part) for r in range(ROW_PARTS)]


def _cast_weight_once(w_ref, wb_ref):
    @pl.when(pl.program_id(1) == 0)
    def _():
        wb_ref[...] = w_ref[...].astype(BF16)


def _proj_plain_kernel(h_ref, w_ref, side_ref, o_ref, side_o_ref, wb_ref):
    side_o_ref[...] = side_ref[...].astype(side_o_ref.dtype)
    _cast_weight_once(w_ref, wb_ref)
    for rows in _row_parts(h_ref.shape[0]):
        o_ref[rows, :] = _bdot(h_ref[rows, :], wb_ref[...]).astype(o_ref.dtype)


def _proj_rope_kernel(h_ref, w_ref, cos_ref, sin_ref, o_ref, wb_ref, *, head_dim, scale):
    _cast_weight_once(w_ref, wb_ref)
    half = head_dim // 2
    for rows in _row_parts(h_ref.shape[0]):
        acc = _bdot(h_ref[rows, :], wb_ref[...])
        cos, sin = cos_ref[rows, :], sin_ref[rows, :]
        for c0 in range(0, acc.shape[1], head_dim):
            x1, x2 = acc[:, c0:c0 + half], acc[:, c0 + half:c0 + head_dim]
            r1 = x1 * cos - x2 * sin
            r2 = x1 * sin + x2 * cos
            if scale is not None:
                r1, r2 = r1 * scale, r2 * scale
            o_ref[rows, c0:c0 + half] = r1.astype(o_ref.dtype)
            o_ref[rows, c0 + half:c0 + head_dim] = r2.astype(o_ref.dtype)


def _proj_swiglu_kernel(h_ref, wg_ref, wu_ref, side_ref, o_ref, side_o_ref, wgb_ref, wub_ref):
    side_o_ref[...] = side_ref[...].astype(side_o_ref.dtype)
    _cast_weight_once(wg_ref, wgb_ref)
    _cast_weight_once(wu_ref, wub_ref)
    for rows in _row_parts(h_ref.shape[0]):
        h = h_ref[rows, :]
        gate = _bdot(h, wgb_ref[...])
        up = _bdot(h, wub_ref[...])
        o_ref[rows, :] = (_silu(gate) * up).astype(o_ref.dtype)


def _side_cast_specs(side_w, layer, n_steps, n_inner):
    _, kdim, n = side_w.shape
    rows = kdim // n_steps
    assert rows * n_steps == kdim and rows % 16 == 0, (kdim, n_steps)
    in_spec = pl.BlockSpec((None, rows, n), lambda j, i: (layer, j * n_inner + i, 0))
    out_spec = pl.BlockSpec((rows, n), lambda j, i: (j * n_inner + i, 0))
    return in_spec, out_spec, jax.ShapeDtypeStruct((kdim, n), BF16)


def _proj(h, w, layer, side_w, *, col0, n_cols, tn, out_dtype, name):
    t, kdim = h.shape
    tm = ROW_TILE
    tn = min(tn, n_cols)
    off = col0 // tn
    grid = (n_cols // tn, t // tm)
    side_in, side_out, side_shape = _side_cast_specs(side_w, layer, grid[0] * grid[1], grid[1])
    return pl.pallas_call(
        _proj_plain_kernel,
        out_shape=(jax.ShapeDtypeStruct((t, n_cols), out_dtype), side_shape),
        grid=grid,
        in_specs=[pl.BlockSpec((tm, kdim), lambda j, i: (i, 0)),
                  pl.BlockSpec((None, kdim, tn), lambda j, i: (layer, 0, j + off)),
                  side_in],
        out_specs=(pl.BlockSpec((tm, tn), lambda j, i: (i, j)), side_out),
        scratch_shapes=[pltpu.VMEM((kdim, tn), BF16)],
        compiler_params=_params("arbitrary", "arbitrary"),
        name=name,
    )(h, w, side_w)


def _proj_rope(h, w, layer, cos, sin, *, col0, n_cols, tn, head_dim, scale, out_dtype, name):
    t, kdim = h.shape
    tm = ROW_TILE
    tn = min(tn, n_cols)
    off = col0 // tn
    kern = functools.partial(_proj_rope_kernel, head_dim=head_dim, scale=scale)
    return pl.pallas_call(
        kern,
        out_shape=jax.ShapeDtypeStruct((t, n_cols), out_dtype),
        grid=(n_cols // tn, t // tm),
        in_specs=[pl.BlockSpec((tm, kdim), lambda j, i: (i, 0)),
                  pl.BlockSpec((None, kdim, tn), lambda j, i: (layer, 0, j + off)),
                  pl.BlockSpec((tm, head_dim // 2), lambda j, i: (i, 0)),
                  pl.BlockSpec((tm, head_dim // 2), lambda j, i: (i, 0))],
        out_specs=pl.BlockSpec((tm, tn), lambda j, i: (i, j)),
        scratch_shapes=[pltpu.VMEM((kdim, tn), BF16)],
        compiler_params=_params("arbitrary", "arbitrary"),
        name=name,
    )(h, w, cos, sin)


def _proj_swiglu(h, w_gu, layer, side_w, *, tn, name):
    t, kdim = h.shape
    d_ff = w_gu.shape[2] // 2
    tm = FFN_ROW_TILE
    n_tiles = d_ff // tn
    grid = (n_tiles, t // tm)
    side_in, side_out, side_shape = _side_cast_specs(side_w, layer, grid[0] * grid[1], grid[1])
    return pl.pallas_call(
        _proj_swiglu_kernel,
        out_shape=(jax.ShapeDtypeStruct((t, d_ff), BF16), side_shape),
        grid=grid,
        in_specs=[pl.BlockSpec((tm, kdim), lambda j, i: (i, 0)),
                  pl.BlockSpec((None, kdim, tn), lambda j, i: (layer, 0, j)),
                  pl.BlockSpec((None, kdim, tn), lambda j, i: (layer, 0, j + n_tiles)),
                  side_in],
        out_specs=(pl.BlockSpec((tm, tn), lambda j, i: (i, j)), side_out),
        scratch_shapes=[pltpu.VMEM((kdim, tn), BF16), pltpu.VMEM((kdim, tn), BF16)],
        compiler_params=_params("arbitrary", "arbitrary"),
        name=name,
    )(h, w_gu, w_gu, side_w)


def _out_proj_kernel(a_ref, w_ref, x_ref, g_ref, *rest, n_k, emit_next):
    if emit_next:
        gn_ref, o_ref, h_ref = rest
    else:
        (o_ref,) = rest
    k = pl.program_id(1)
    tk = a_ref.shape[1]

    def accumulate(kk, rows):
        a = a_ref[rows, :]
        for n in range(0, o_ref.shape[1], OUT_COL_CHUNK):
            cols = slice(n, n + OUT_COL_CHUNK)
            part = _bdot(a, w_ref[kk * tk:(kk + 1) * tk, cols])
            if kk == 0:
                o_ref[rows, cols] = part
            else:
                o_ref[rows, cols] += part

    def epilogue(rows):
        x_new = x_ref[rows, :] + _rms(o_ref[rows, :], g_ref[...])
        o_ref[rows, :] = x_new
        if emit_next:
            h_ref[rows, :] = _rms(x_new, gn_ref[...]).astype(h_ref.dtype)

    for kk in range(n_k):
        @pl.when(k == kk)
        def _(kk=kk):
            for rows in _row_parts(a_ref.shape[0]):
                accumulate(kk, rows)
                if kk == n_k - 1:
                    epilogue(rows)


def _out_proj(a, w_bf16, x, g_post, g_next, *, n_k, name):
    t, kdim = a.shape
    d = w_bf16.shape[1]
    tm = OUT_ROW_TILE
    tk = kdim // n_k
    emit_next = g_next is not None
    kern = functools.partial(_out_proj_kernel, n_k=n_k, emit_next=emit_next)
    row_spec = pl.BlockSpec((tm, d), lambda i, k: (i, 0))
    gain_spec = pl.BlockSpec((1, d), lambda i, k: (0, 0))
    operands = [a, w_bf16, x, g_post]
    in_specs = [pl.BlockSpec((tm, tk), lambda i, k: (i, k)),
                pl.BlockSpec((kdim, d), lambda i, k: (0, 0), pipeline_mode=pl.Buffered(1)),
                row_spec, gain_spec]
    out_shape = [jax.ShapeDtypeStruct((t, d), F32)]
    out_specs = [row_spec]
    if emit_next:
        operands.append(g_next)
        in_specs.append(gain_spec)
        out_shape.append(jax.ShapeDtypeStruct((t, d), BF16))
        out_specs.append(row_spec)
    res = pl.pallas_call(
        kern,
        out_shape=tuple(out_shape),
        grid=(t // tm, n_k),
        in_specs=in_specs,
        out_specs=tuple(out_specs),
        compiler_params=_params("arbitrary", "arbitrary"),
        name=name,
    )(*operands)
    return (res[0], res[1]) if emit_next else (res[0], None)


def _retention_kernel(lg_ref, q_ref, k_ref, v_ref, h_ref, wg_ref, gn_ref, o_ref,
                      state_ref, wgb_ref, gate_ref, *, heads_per_step):
    c_len = RET_CHUNK
    dk, dv = RET_QK_DIM, RET_V_DIM
    n_chunks = q_ref.shape[0] // c_len
    first_block_of_heads = jnp.logical_and(pl.program_id(1) == 0, pl.program_id(2) == 0)

    @pl.when(first_block_of_heads)
    def _():
        wgb_ref[...] = wg_ref[...].astype(BF16)

    @pl.when(pl.program_id(2) == 0)
    def _():
        state_ref[...] = jnp.zeros_like(state_ref)

    row = lax.broadcasted_iota(jnp.int32, (c_len, c_len), 0)
    col = lax.broadcasted_iota(jnp.int32, (c_len, c_len), 1)
    rel = (row - col).astype(F32)
    causal = rel >= 0
    rel_pos = jnp.where(causal, rel, 0.0)
    idx = lax.broadcasted_iota(jnp.int32, (c_len, 1), 0).astype(F32)
    decays = []
    for g in range(heads_per_step):
        lg = jnp.full((1, 1), lg_ref[pl.program_id(0) * heads_per_step + g], F32)
        decays.append((jnp.where(causal, jnp.exp(lg * rel_pos), 0.0),
                       jnp.exp(lg * (idx + 1.0)),
                       jnp.exp(lg * (c_len - 1.0 - idx)),
                       jnp.exp(lg * c_len)))

    for rows in _row_parts(h_ref.shape[0]):
        for g in range(heads_per_step):
            v_cols = slice(g * dv, (g + 1) * dv)
            gate_ref[rows, v_cols] = _silu(_bdot(h_ref[rows, :], wgb_ref[:, v_cols])).astype(gate_ref.dtype)

    for c in range(n_chunks):
        rows = slice(c * c_len, (c + 1) * c_len)
        for g in range(heads_per_step):
            decay_in, decay_q, decay_k, decay_chunk = decays[g]
            qk_cols = slice(g * dk, (g + 1) * dk)
            v_cols = slice(g * dv, (g + 1) * dv)
            q = q_ref[rows, qk_cols]
            k = k_ref[rows, qk_cols]
            v = v_ref[rows, v_cols]
            scores = lax.dot_general(q, k.astype(BF16), (((1,), (1,)), ((), ())),
                                     preferred_element_type=F32) * decay_in
            state = state_ref[g]
            y = _bdot(scores.astype(BF16), v) + _bdot(q, state.astype(BF16)) * decay_q
            kd_t = (k * decay_k).T.astype(BF16)
            state_ref[g] = state * decay_chunk + _bdot(kd_t, v)

            mu = jnp.mean(y, axis=-1, keepdims=True)
            yc = y - mu
            var = jnp.mean(yc * yc, axis=-1, keepdims=True)
            yn = yc * lax.rsqrt(var + NORM_EPS) * gn_ref[:, v_cols]
            o_ref[rows, v_cols] = (gate_ref[rows, v_cols] * yn).astype(o_ref.dtype)


def _retention_core(q, k, v, h, w_in, layer, gate_col0, gn_g, log_gamma, *, batch, seq):
    t, d = h.shape
    dk, dv = RET_QK_DIM, RET_V_DIM
    heads = v.shape[1] // dv
    hp = RET_HEADS_PER_STEP
    lb = RET_SEQ_BLOCK
    nb = seq // lb
    gate_off = gate_col0 // (hp * dv)
    kern = functools.partial(_retention_kernel, heads_per_step=hp)
    return pl.pallas_call(
        kern,
        out_shape=jax.ShapeDtypeStruct((t, heads * dv), BF16),
        grid=(heads // hp, batch, nb),
        in_specs=[pl.BlockSpec(memory_space=pltpu.SMEM),
                  pl.BlockSpec((lb, hp * dk), lambda h_, b, s: (b * nb + s, h_)),
                  pl.BlockSpec((lb, hp * dk), lambda h_, b, s: (b * nb + s, h_)),
                  pl.BlockSpec((lb, hp * dv), lambda h_, b, s: (b * nb + s, h_)),
                  pl.BlockSpec((lb, d), lambda h_, b, s: (b * nb + s, 0)),
                  pl.BlockSpec((None, d, hp * dv), lambda h_, b, s: (layer, 0, gate_off + h_)),
                  pl.BlockSpec((1, hp * dv), lambda h_, b, s: (0, h_))],
        out_specs=pl.BlockSpec((lb, hp * dv), lambda h_, b, s: (b * nb + s, h_)),
        scratch_shapes=[pltpu.VMEM((hp, dk, dv), F32),
                        pltpu.VMEM((d, hp * dv), BF16),
                        pltpu.VMEM((lb, hp * dv), BF16)],
        compiler_params=_params("arbitrary", "arbitrary", "arbitrary"),
        name="retention_core",
    )(log_gamma, q, k, v, h, w_in, gn_g)


def _mla_rope_table_kernel(pos_ref, inv_ref, cos_mask_ref, sin_sign_ref, tcos_ref, tsin_ref):
    ang = pos_ref[...].astype(F32) * inv_ref[...]
    tcos_ref[...] = jnp.cos(ang) * cos_mask_ref[...]
    tsin_ref[...] = jnp.sin(ang) * sin_sign_ref[...]


def _mla_rope_tables(pos_col):
    t = pos_col.shape[0]
    half = MLA_ROPE // 2
    inv = ROPE_THETA ** (-jnp.arange(0, MLA_ROPE, 2, dtype=F32) / MLA_ROPE)
    ones, zeros = jnp.ones((half,), F32), jnp.zeros((half,), F32)
    inv_row = jnp.concatenate([inv, inv, zeros, zeros]).reshape(1, LANES)
    cos_mask = jnp.concatenate([ones, ones, zeros, zeros]).reshape(1, LANES)
    sin_sign = jnp.concatenate([-ones, ones, zeros, zeros]).reshape(1, LANES)
    tm = ROW_TILE
    row = pl.BlockSpec((1, LANES), lambda i: (0, 0))
    return pl.pallas_call(
        _mla_rope_table_kernel,
        out_shape=(jax.ShapeDtypeStruct((t, LANES), F32), jax.ShapeDtypeStruct((t, LANES), F32)),
        grid=(t // tm,),
        in_specs=[pl.BlockSpec((tm, 1), lambda i: (i, 0)), row, row, row],
        out_specs=(pl.BlockSpec((tm, LANES), lambda i: (i, 0)),
                   pl.BlockSpec((tm, LANES), lambda i: (i, 0))),
        compiler_params=_params("parallel"),
        name="mla_rope_tables",
    )(pos_col, inv_row, cos_mask, sin_sign)


def _mla_in_kernel(h_ref, w_ref, gq_ref, gkv_ref, tcos_ref, tsin_ref,
                   cq_ref, ckv_ref, kr_ref, wb_ref, *, q_lora, kv_lora):
    @pl.when(pl.program_id(0) == 0)
    def _():
        wb_ref[...] = w_ref[...].astype(BF16)

    for rows in _row_parts(h_ref.shape[0]):
        c = _bdot(h_ref[rows, :], wb_ref[...])
        cq_ref[rows, :] = _rms(c[:, :q_lora], gq_ref[...]).astype(cq_ref.dtype)
        ckv_ref[rows, :] = _rms(c[:, q_lora:q_lora + kv_lora], gkv_ref[...]).astype(ckv_ref.dtype)
        kr = c[:, q_lora + kv_lora:]
        half = kr.shape[1] // 2
        v = jnp.concatenate([kr, jnp.zeros_like(kr)], axis=-1)
        lane = lax.broadcasted_iota(jnp.int32, v.shape, 1)
        swapped = jnp.where(lane < half, pltpu.roll(v, LANES - half, 1), pltpu.roll(v, half, 1))
        swapped = jnp.where(lane < 2 * half, swapped, 0.0)
        kr_ref[rows, :] = (v * tcos_ref[rows, :] + swapped * tsin_ref[rows, :]).astype(kr_ref.dtype)


def _mla_in(h, w, layer, gq, gkv, tcos, tsin):
    t, d = h.shape
    q_lora, kv_lora = gq.shape[1], gkv.shape[1]
    n_in = w.shape[2]
    tm = ROW_TILE
    kern = functools.partial(_mla_in_kernel, q_lora=q_lora, kv_lora=kv_lora)
    return pl.pallas_call(
        kern,
        out_shape=(jax.ShapeDtypeStruct((t, q_lora), BF16),
                   jax.ShapeDtypeStruct((t, kv_lora), BF16),
                   jax.ShapeDtypeStruct((t, LANES), BF16)),
        grid=(t // tm,),
        in_specs=[pl.BlockSpec((tm, d), lambda i: (i, 0)),
                  pl.BlockSpec((None, d, n_in), lambda i: (layer, 0, 0)),
                  pl.BlockSpec((1, q_lora), lambda i: (0, 0)),
                  pl.BlockSpec((1, kv_lora), lambda i: (0, 0)),
                  pl.BlockSpec((tm, LANES), lambda i: (i, 0)),
                  pl.BlockSpec((tm, LANES), lambda i: (i, 0))],
        out_specs=(pl.BlockSpec((tm, q_lora), lambda i: (i, 0)),
                   pl.BlockSpec((tm, kv_lora), lambda i: (i, 0)),
                   pl.BlockSpec((tm, LANES), lambda i: (i, 0))),
        scratch_shapes=[pltpu.VMEM((d, n_in), BF16)],
        compiler_params=_params("arbitrary"),
        name="mla_in",
    )(h, w, gq, gkv, tcos, tsin)


def _mla_q_weight_kernel(w_ref, o_ref):
    half = MLA_ROPE // 2
    w = w_ref[...]
    for t in range(o_ref.shape[0]):
        o = t * MLA_QD
        nope = w[:, o:o + MLA_NOPE]
        x1 = w[:, o + MLA_NOPE:o + MLA_NOPE + half]
        x2 = w[:, o + MLA_NOPE + half:o + MLA_QD]
        o_ref[t] = jnp.concatenate([nope, x1, x2, x2, x1], axis=-1).astype(o_ref.dtype)


def _mla_q_weights(w_uq, layer):
    q_lora = w_uq.shape[1]
    heads = w_uq.shape[2] // MLA_QD
    hp = 2
    return pl.pallas_call(
        _mla_q_weight_kernel,
        out_shape=jax.ShapeDtypeStruct((heads, q_lora, MLA_PAD), BF16),
        grid=(heads // hp,),
        in_specs=[pl.BlockSpec((None, q_lora, hp * MLA_QD), lambda h: (layer, 0, h))],
        out_specs=pl.BlockSpec((hp, q_lora, MLA_PAD), lambda h: (h, 0, 0)),
        compiler_params=_params("parallel"),
        name="mla_q_weights",
    )(w_uq)


def _mla_q_kernel(cq_ref, w_ref, tcos_ref, tsin_ref, o_ref, *, scale):
    cq = cq_ref[...]
    tcos, tsin = tcos_ref[...], tsin_ref[...]
    for g in range(o_ref.shape[0]):
        res = _bdot(cq, w_ref[g])
        rot = res[:, MLA_NOPE:]
        rope = rot * tcos + pltpu.roll(rot, LANES // 2, 1) * tsin
        o_ref[g, :, :MLA_NOPE] = (res[:, :MLA_NOPE] * scale).astype(o_ref.dtype)
        o_ref[g, :, MLA_NOPE:] = (rope * scale).astype(o_ref.dtype)


def _mla_q(cq, w_q, tcos, tsin):
    t, q_lora = cq.shape
    heads = w_q.shape[0]
    tm = ROW_TILE
    hp = min(MLA_Q_HEADS_PER_STEP, heads)
    kern = functools.partial(_mla_q_kernel, scale=MLA_QD ** -0.5 * LOG2_E)
    return pl.pallas_call(
        kern,
        out_shape=jax.ShapeDtypeStruct((heads, t, MLA_PAD), BF16),
        grid=(t // tm, heads // hp),
        in_specs=[pl.BlockSpec((tm, q_lora), lambda i, h: (i, 0)),
                  pl.BlockSpec((hp, q_lora, MLA_PAD), lambda i, h: (h, 0, 0)),
                  pl.BlockSpec((tm, LANES), lambda i, h: (i, 0)),
                  pl.BlockSpec((tm, LANES), lambda i, h: (i, 0))],
        out_specs=pl.BlockSpec((hp, tm, MLA_PAD), lambda i, h: (h, i, 0)),
        compiler_params=_params("parallel", "parallel"),
        name="mla_q",
    )(cq, w_q, tcos, tsin)


def _mla_kv_kernel(ckv_ref, w_ref, side_ref, k_ref, v_ref, side_o_ref):
    side_o_ref[...] = side_ref[...].astype(side_o_ref.dtype)
    ckv = ckv_ref[...]
    per_head = MLA_NOPE + MLA_V
    for g in range(k_ref.shape[0]):
        kv = _bdot(ckv, w_ref[:, g * per_head:(g + 1) * per_head].astype(BF16))
        k_ref[g] = kv[:, :MLA_NOPE].astype(k_ref.dtype)
        v_ref[g] = kv[:, MLA_NOPE:].astype(v_ref.dtype)


def _mla_kv(ckv, w_ukv, layer, side_w):
    t, kv_lora = ckv.shape
    per_head = MLA_NOPE + MLA_V
    heads = w_ukv.shape[2] // per_head
    tm = ROW_TILE
    hp = min(MLA_HEADS_PER_STEP, heads)
    grid = (t // tm, heads // hp)
    side_in, side_out, side_shape = _side_cast_specs(side_w, layer, grid[0] * grid[1], grid[1])
    return pl.pallas_call(
        _mla_kv_kernel,
        out_shape=(jax.ShapeDtypeStruct((heads, t, MLA_NOPE), BF16),
                   jax.ShapeDtypeStruct((heads, t, MLA_V), BF16),
                   side_shape),
        grid=grid,
        in_specs=[pl.BlockSpec((tm, kv_lora), lambda i, h: (i, 0)),
                  pl.BlockSpec((None, kv_lora, hp * per_head), lambda i, h: (layer, 0, h)),
                  side_in],
        out_specs=(pl.BlockSpec((hp, tm, MLA_NOPE), lambda i, h: (h, i, 0)),
                   pl.BlockSpec((hp, tm, MLA_V), lambda i, h: (h, i, 0)),
                   side_out),
        compiler_params=_params("arbitrary", "arbitrary"),
        name="mla_kv",
    )(ckv, w_ukv, side_w)


def _flash_kernel(q_ref, kn_ref, kr_ref, vv_ref, o_ref, k_ref, v_ref, sa_ref, sb_ref, m_ref, acc_ref):
    n_heads, seq, _ = q_ref.shape
    tile = ATTN_TILE
    nq = seq // tile
    tiles = [(qi, ki) for qi in range(nq) for ki in range(qi + 1)]
    s_bufs = (sa_ref, sb_ref)

    def assemble(ki):
        for g in range(n_heads):
            k_ref[g, rows(ki), :MLA_NOPE] = kn_ref[g, rows(ki), :]
            k_ref[g, rows(ki), MLA_NOPE:] = kr_ref[rows(ki), :]
            v_ref[g, rows(ki), :MLA_V] = vv_ref[g, rows(ki), :]
            v_ref[g, rows(ki), MLA_V:] = jnp.ones((tile, v_ref.shape[2] - MLA_V), v_ref.dtype)

    def rows(i):
        return slice(i * tile, (i + 1) * tile)

    def row_parts(qi, ki):
        if ki != qi:
            return [(0, tile, tile)]
        return [(0, tile // 2, tile // 2), (tile // 2, tile, tile)]

    def scores_into(s_ref, qi, ki):
        for g in range(n_heads):
            for r0, r1, n_kv in row_parts(qi, ki):
                q = q_ref[g, qi * tile + r0:qi * tile + r1, :]
                k = k_ref[g, ki * tile:ki * tile + n_kv, :]
                s_ref[g, r0:r1, :n_kv] = lax.dot_general(q, k, (((1,), (1,)), ((), ())),
                                                         preferred_element_type=F32)

    def consume(s_ref, qi, ki):
        slot = qi % 2
        for g in range(n_heads):
            for r0, r1, n_kv in row_parts(qi, ki):
                s = s_ref[g, r0:r1, :n_kv]
                if ki == qi:
                    row = lax.broadcasted_iota(jnp.int32, s.shape, 0) + r0
                    col = lax.broadcasted_iota(jnp.int32, s.shape, 1)
                    s = jnp.where(col <= row, s, NEG_BIG)
                m_cur = jnp.max(s, axis=-1, keepdims=True)
                if ki == 0:
                    m_new = m_cur
                else:
                    m_prev = m_ref[slot, g, r0:r1]
                    m_new = jnp.maximum(m_prev, m_cur)
                    alpha = jnp.exp2(m_prev - m_new)
                p = jnp.exp2(s - m_new).astype(BF16)
                pv = _bdot(p, v_ref[g, ki * tile:ki * tile + n_kv, :])
                acc = pv if ki == 0 else alpha * acc_ref[slot, g, r0:r1] + pv
                if ki == qi:
                    out = (acc[:, :MLA_V] / acc[:, MLA_V:]).astype(o_ref.dtype)
                    o_ref[qi * tile + r0:qi * tile + r1, g * MLA_V:(g + 1) * MLA_V] = out
                else:
                    acc_ref[slot, g, r0:r1] = acc
                    m_ref[slot, g, r0:r1] = m_new

    def step(n):
        qi, ki = tiles[n]
        if n + 1 < len(tiles):
            next_qi, next_ki = tiles[n + 1]
            if next_ki == next_qi:
                assemble(next_ki)
            scores_into(s_bufs[(n + 1) % 2], next_qi, next_ki)
        consume(s_bufs[n % 2], qi, ki)

    def steps(first):
        for n in range(first, min(first + ATTN_STEPS_PER_REGION, len(tiles))):
            step(n)

    one_trip = jnp.minimum(pl.program_id(0) + 1, 1)
    assemble(0)
    scores_into(s_bufs[0], *tiles[0])
    for first in range(0, len(tiles), ATTN_STEPS_PER_REGION):
        lax.fori_loop(0, one_trip, lambda _, carry, first=first: (steps(first), carry)[1], 0)


def _flash_attention(q, k_nope, k_rope, v, *, batch, seq):
    heads, t, _ = q.shape
    tile = ATTN_TILE
    hp = ATTN_HEADS_PER_STEP
    narrow_block = pl.BlockSpec((hp, seq, MLA_V), lambda b, h: (h, b, 0))
    return pl.pallas_call(
        _flash_kernel,
        out_shape=jax.ShapeDtypeStruct((t, heads * MLA_V), BF16),
        grid=(batch, heads // hp),
        in_specs=[pl.BlockSpec((hp, seq, MLA_PAD), lambda b, h: (h, b, 0)),
                  narrow_block,
                  pl.BlockSpec((seq, LANES), lambda b, h: (b, 0)),
                  narrow_block],
        out_specs=pl.BlockSpec((seq, hp * MLA_V), lambda b, h: (b, h)),
        scratch_shapes=[pltpu.VMEM((hp, seq, MLA_PAD), BF16),
                        pltpu.VMEM((hp, seq, 2 * MLA_V), BF16),
                        pltpu.VMEM((hp, tile, tile), F32),
                        pltpu.VMEM((hp, tile, tile), F32),
                        pltpu.VMEM((2, hp, tile, 1), F32),
                        pltpu.VMEM((2, hp, tile, 2 * MLA_V), F32)],
        compiler_params=_params("parallel", "parallel"),
        name="mla_flash_attention",
    )(q, k_nope, k_rope, v)


def _retention_layer(x, h, g_post, g_next, w_in, gn_g, w_out, layer, cos, sin, log_gamma, *, batch, seq):
    d = x.shape[1]
    dk, dv = RET_QK_DIM, RET_V_DIM
    heads = d // dk
    n_q = heads * dk
    n_v = heads * dv
    tn = PROJ_COL_TILE
    q = _proj_rope(h, w_in, layer, cos, sin, col0=0, n_cols=n_q, tn=tn, head_dim=dk,
                   scale=None, out_dtype=BF16, name="ret_q_proj")
    k = _proj_rope(h, w_in, layer, cos, sin, col0=n_q, n_cols=n_q, tn=tn, head_dim=dk,
                   scale=dk ** -0.5, out_dtype=F32, name="ret_k_proj")
    v, w_out_b = _proj(h, w_in, layer, w_out, col0=2 * n_q, n_cols=n_v, tn=tn, out_dtype=BF16,
                       name="ret_v_proj")
    y = _retention_core(q, k, v, h, w_in, layer, 2 * n_q + n_v, gn_g, log_gamma, batch=batch, seq=seq)
    return _out_proj(y, w_out_b, x, g_post, g_next, n_k=1, name="ret_out_proj")


def _mla_layer(x, h, g_post, g_next, w_in, g_q, g_kv, w_uq, w_ukv, w_out, layer, tcos, tsin, *, batch, seq):
    cq, ckv, k_rope = _mla_in(h, w_in, layer, g_q, g_kv, tcos, tsin)
    q = _mla_q(cq, _mla_q_weights(w_uq, layer), tcos, tsin)
    k_nope, v, w_out_b = _mla_kv(ckv, w_ukv, layer, w_out)
    o = _flash_attention(q, k_nope, k_rope, v, batch=batch, seq=seq)
    return _out_proj(o, w_out_b, x, g_post, g_next, n_k=1, name="mla_out_proj")


def _ffn(x, h, g_post, g_next, w_gu, w_down, layer):
    act, w_down_b = _proj_swiglu(h, w_gu, layer, w_down, tn=FFN_COL_TILE, name="ffn_gate_up")
    return _out_proj(act, w_down_b, x, g_post, g_next, n_k=FFN_DOWN_K_PARTS, name="ffn_down_proj")


def kernel(x, positions, norm_mix_pre, norm_mix_post, norm_ffn_pre, norm_ffn_post, ret_w_in, ret_gn_g, ret_w_out, mla_w_in, mla_g_q, mla_g_kv, mla_w_uq, mla_w_ukv, mla_w_out, ffn_w_gu, ffn_w_down):
    batch, seq, d = x.shape
    depth = norm_mix_pre.shape[0]
    t = batch * seq
    xf = x.reshape(t, d)
    pos_col = positions.reshape(t, 1)
    cos_r, sin_r = _rope_tables(pos_col, RET_QK_DIM)
    tcos_m, tsin_m = _mla_rope_tables(pos_col)
    ret_heads = d // RET_QK_DIM
    log_gamma = jnp.log1p(-jnp.exp2(-5.0 - jnp.arange(ret_heads, dtype=F32)))

    def row(a, i):
        return a[i].reshape(1, -1)

    h = _prenorm(xf, row(norm_mix_pre, 0))
    for i in range(depth):
        j = i // N_MIXERS
        g_ffn_pre = row(norm_ffn_pre, i)
        g_next_mix = row(norm_mix_pre, i + 1) if i + 1 < depth else None
        if i % N_MIXERS == 0:
            xf, h = _retention_layer(xf, h, row(norm_mix_post, i), g_ffn_pre, ret_w_in, row(ret_gn_g, j),
                                     ret_w_out, j, cos_r, sin_r, log_gamma, batch=batch, seq=seq)
        else:
            xf, h = _mla_layer(xf, h, row(norm_mix_post, i), g_ffn_pre, mla_w_in, row(mla_g_q, j),
                               row(mla_g_kv, j), mla_w_uq, mla_w_ukv, mla_w_out, j, tcos_m, tsin_m,
                               batch=batch, seq=seq)
        xf, h = _ffn(xf, h, row(norm_ffn_post, i), g_next_mix, ffn_w_gu, ffn_w_down, i)
    return xf.reshape(batch, seq, d)
```

```python
import functools

import jax
import jax.numpy as jnp
from jax import lax
from jax.experimental import pallas as pl
from jax.experimental.pallas import tpu as pltpu

F32 = jnp.float32
BF16 = jnp.bfloat16

RET_QK_DIM = 256
RET_V_DIM = 2 * RET_QK_DIM
MLA_NOPE = 128
MLA_ROPE = 64
MLA_V = 128
MLA_QD = MLA_NOPE + MLA_ROPE
MLA_PAD = 256
LANES = 128
ROPE_THETA = 10000.0
NORM_EPS = 1e-6
LOG2_E = 1.4426950408889634
N_MIXERS = 2

V7X_VMEM_BYTES = 64 * 1024 * 1024
VMEM_LIMIT_BYTES = V7X_VMEM_BYTES - 8 * 1024 * 1024

ROW_TILE = 1024
FFN_ROW_TILE = 2048
ROW_PARTS = 2
PROJ_COL_TILE = 1024
FFN_COL_TILE = 512
OUT_ROW_TILE = 512
FFN_DOWN_K_PARTS = 2
RET_CHUNK = 256
RET_SEQ_BLOCK = 1024
RET_HEADS_PER_STEP = 2
MLA_HEADS_PER_STEP = 8
MLA_Q_HEADS_PER_STEP = 16
ATTN_TILE = 512
ATTN_HEADS_PER_STEP = 4
ATTN_STEPS_PER_REGION = 4
OUT_COL_CHUNK = 512
NEG_BIG = -1e30


def _params(*sem):
    return pltpu.CompilerParams(dimension_semantics=sem, vmem_limit_bytes=VMEM_LIMIT_BYTES)


def _rms(x, g):
    return x * lax.rsqrt(jnp.mean(x * x, axis=-1, keepdims=True) + NORM_EPS) * g


def _bdot(a, b):
    return jnp.dot(a, b, preferred_element_type=F32)


def _silu(x):
    return x * jax.nn.sigmoid(x)


def _rope_table_kernel(pos_ref, inv_ref, cos_ref, sin_ref):
    ang = pos_ref[...].astype(F32) * inv_ref[...]
    cos_ref[...] = jnp.cos(ang)
    sin_ref[...] = jnp.sin(ang)


def _rope_tables(pos_col, d):
    t = pos_col.shape[0]
    f = d // 2
    inv = (ROPE_THETA ** (-jnp.arange(0, d, 2, dtype=F32) / d)).reshape(1, f)
    tm = ROW_TILE
    return pl.pallas_call(
        _rope_table_kernel,
        out_shape=(jax.ShapeDtypeStruct((t, f), F32), jax.ShapeDtypeStruct((t, f), F32)),
        grid=(t // tm,),
        in_specs=[pl.BlockSpec((tm, 1), lambda i: (i, 0)),
                  pl.BlockSpec((1, f), lambda i: (0, 0))],
        out_specs=(pl.BlockSpec((tm, f), lambda i: (i, 0)),
                   pl.BlockSpec((tm, f), lambda i: (i, 0))),
        compiler_params=_params("parallel"),
        name=f"rope_tables_{d}",
    )(pos_col, inv)


def _prenorm_kernel(x_ref, g_ref, h_ref):
    h_ref[...] = _rms(x_ref[...], g_ref[...]).astype(h_ref.dtype)


def _prenorm(x, g):
    t, d = x.shape
    tm = ROW_TILE
    return pl.pallas_call(
        _prenorm_kernel,
        out_shape=jax.ShapeDtypeStruct((t, d), BF16),
        grid=(t // tm,),
        in_specs=[pl.BlockSpec((tm, d), lambda i: (i, 0)),
                  pl.BlockSpec((1, d), lambda i: (0, 0))],
        out_specs=pl.BlockSpec((tm, d), lambda i: (i, 0)),
        compiler_params=_params("parallel"),
        name="prenorm",
    )(x, g)


def _row_parts(n_rows):
    part = n_rows // ROW_PARTS
    return [slice(r * part, (r + 1) * part) for r in range(ROW_PARTS)]


def _cast_weight_once(w_ref, wb_ref):
    @pl.when(pl.program_id(1) == 0)
    def _():
        wb_ref[...] = w_ref[...].astype(BF16)


def _proj_plain_kernel(h_ref, w_ref, side_ref, o_ref, side_o_ref, wb_ref):
    side_o_ref[...] = side_ref[...].astype(side_o_ref.dtype)
    _cast_weight_once(w_ref, wb_ref)
    for rows in _row_parts(h_ref.shape[0]):
        o_ref[rows, :] = _bdot(h_ref[rows, :], wb_ref[...]).astype(o_ref.dtype)


def _proj_rope_kernel(h_ref, w_ref, cos_ref, sin_ref, o_ref, wb_ref, *, head_dim, scale):
    _cast_weight_once(w_ref, wb_ref)
    half = head_dim // 2
    for rows in _row_parts(h_ref.shape[0]):
        acc = _bdot(h_ref[rows, :], wb_ref[...])
        cos, sin = cos_ref[rows, :], sin_ref[rows, :]
        for c0 in range(0, acc.shape[1], head_dim):
            x1, x2 = acc[:, c0:c0 + half], acc[:, c0 + half:c0 + head_dim]
            r1 = x1 * cos - x2 * sin
            r2 = x1 * sin + x2 * cos
            if scale is not None:
                r1, r2 = r1 * scale, r2 * scale
            o_ref[rows, c0:c0 + half] = r1.astype(o_ref.dtype)
            o_ref[rows, c0 + half:c0 + head_dim] = r2.astype(o_ref.dtype)


def _proj_swiglu_kernel(h_ref, wg_ref, wu_ref, side_ref, o_ref, side_o_ref, wgb_ref, wub_ref):
    side_o_ref[...] = side_ref[...].astype(side_o_ref.dtype)
    _cast_weight_once(wg_ref, wgb_ref)
    _cast_weight_once(wu_ref, wub_ref)
    for rows in _row_parts(h_ref.shape[0]):
        h = h_ref[rows, :]
        gate = _bdot(h, wgb_ref[...])
        up = _bdot(h, wub_ref[...])
        o_ref[rows, :] = (_silu(gate) * up).astype(o_ref.dtype)


def _side_cast_specs(side_w, layer, n_steps, n_inner):
    _, kdim, n = side_w.shape
    rows = kdim // n_steps
    assert rows * n_steps == kdim and rows % 16 == 0, (kdim, n_steps)
    in_spec = pl.BlockSpec((None, rows, n), lambda j, i: (layer, j * n_inner + i, 0))
    out_spec = pl.BlockSpec((rows, n), lambda j, i: (j * n_inner + i, 0))
    return in_spec, out_spec, jax.ShapeDtypeStruct((kdim, n), BF16)


def _proj(h, w, layer, side_w, *, col0, n_cols, tn, out_dtype, name):
    t, kdim = h.shape
    tm = ROW_TILE
    tn = min(tn, n_cols)
    off = col0 // tn
    grid = (n_cols // tn, t // tm)
    side_in, side_out, side_shape = _side_cast_specs(side_w, layer, grid[0] * grid[1], grid[1])
    return pl.pallas_call(
        _proj_plain_kernel,
        out_shape=(jax.ShapeDtypeStruct((t, n_cols), out_dtype), side_shape),
        grid=grid,
        in_specs=[pl.BlockSpec((tm, kdim), lambda j, i: (i, 0)),
                  pl.BlockSpec((None, kdim, tn), lambda j, i: (layer, 0, j + off)),
                  side_in],
        out_specs=(pl.BlockSpec((tm, tn), lambda j, i: (i, j)), side_out),
        scratch_shapes=[pltpu.VMEM((kdim, tn), BF16)],
        compiler_params=_params("arbitrary", "arbitrary"),
        name=name,
    )(h, w, side_w)


def _proj_rope(h, w, layer, cos, sin, *, col0, n_cols, tn, head_dim, scale, out_dtype, name):
    t, kdim = h.shape
    tm = ROW_TILE
    tn = min(tn, n_cols)
    off = col0 // tn
    kern = functools.partial(_proj_rope_kernel, head_dim=head_dim, scale=scale)
    return pl.pallas_call(
        kern,
        out_shape=jax.ShapeDtypeStruct((t, n_cols), out_dtype),
        grid=(n_cols // tn, t // tm),
        in_specs=[pl.BlockSpec((tm, kdim), lambda j, i: (i, 0)),
                  pl.BlockSpec((None, kdim, tn), lambda j, i: (layer, 0, j + off)),
                  pl.BlockSpec((tm, head_dim // 2), lambda j, i: (i, 0)),
                  pl.BlockSpec((tm, head_dim // 2), lambda j, i: (i, 0))],
        out_specs=pl.BlockSpec((tm, tn), lambda j, i: (i, j)),
        scratch_shapes=[pltpu.VMEM((kdim, tn), BF16)],
        compiler_params=_params("arbitrary", "arbitrary"),
        name=name,
    )(h, w, cos, sin)


def _proj_swiglu(h, w_gu, layer, side_w, *, tn, name):
    t, kdim = h.shape
    d_ff = w_gu.shape[2] // 2
    tm = FFN_ROW_TILE
    n_tiles = d_ff // tn
    grid = (n_tiles, t // tm)
    side_in, side_out, side_shape = _side_cast_specs(side_w, layer, grid[0] * grid[1], grid[1])
    return pl.pallas_call(
        _proj_swiglu_kernel,
        out_shape=(jax.ShapeDtypeStruct((t, d_ff), BF16), side_shape),
        grid=grid,
        in_specs=[pl.BlockSpec((tm, kdim), lambda j, i: (i, 0)),
                  pl.BlockSpec((None, kdim, tn), lambda j, i: (layer, 0, j)),
                  pl.BlockSpec((None, kdim, tn), lambda j, i: (layer, 0, j + n_tiles)),
                  side_in],
        out_specs=(pl.BlockSpec((tm, tn), lambda j, i: (i, j)), side_out),
        scratch_shapes=[pltpu.VMEM((kdim, tn), BF16), pltpu.VMEM((kdim, tn), BF16)],
        compiler_params=_params("arbitrary", "arbitrary"),
        name=name,
    )(h, w_gu, w_gu, side_w)


def _out_proj_kernel(a_ref, w_ref, x_ref, g_ref, *rest, n_k, emit_next):
    if emit_next:
        gn_ref, o_ref, h_ref = rest
    else:
        (o_ref,) = rest
    k = pl.program_id(1)
    tk = a_ref.shape[1]

    def accumulate(kk, rows):
        a = a_ref[rows, :]
        for n in range(0, o_ref.shape[1], OUT_COL_CHUNK):
            cols = slice(n, n + OUT_COL_CHUNK)
            part = _bdot(a, w_ref[kk * tk:(kk + 1) * tk, cols])
            if kk == 0:
                o_ref[rows, cols] = part
            else:
                o_ref[rows, cols] += part

    def epilogue(rows):
        x_new = x_ref[rows, :] + _rms(o_ref[rows, :], g_ref[...])
        o_ref[rows, :] = x_new
        if emit_next:
            h_ref[rows, :] = _rms(x_new, gn_ref[...]).astype(h_ref.dtype)

    for kk in range(n_k):
        @pl.when(k == kk)
        def _(kk=kk):
            for rows in _row_parts(a_ref.shape[0]):
                accumulate(kk, rows)
                if kk == n_k - 1:
                    epilogue(rows)


def _out_proj(a, w_bf16, x, g_post, g_next, *, n_k, name):
    t, kdim = a.shape
    d = w_bf16.shape[1]
    tm = OUT_ROW_TILE
    tk = kdim // n_k
    emit_next = g_next is not None
    kern = functools.partial(_out_proj_kernel, n_k=n_k, emit_next=emit_next)
    row_spec = pl.BlockSpec((tm, d), lambda i, k: (i, 0))
    gain_spec = pl.BlockSpec((1, d), lambda i, k: (0, 0))
    operands = [a, w_bf16, x, g_post]
    in_specs = [pl.BlockSpec((tm, tk), lambda i, k: (i, k)),
                pl.BlockSpec((kdim, d), lambda i, k: (0, 0), pipeline_mode=pl.Buffered(1)),
                row_spec, gain_spec]
    out_shape = [jax.ShapeDtypeStruct((t, d), F32)]
    out_specs = [row_spec]
    if emit_next:
        operands.append(g_next)
        in_specs.append(gain_spec)
        out_shape.append(jax.ShapeDtypeStruct((t, d), BF16))
        out_specs.append(row_spec)
    res = pl.pallas_call(
        kern,
        out_shape=tuple(out_shape),
        grid=(t // tm, n_k),
        in_specs=in_specs,
        out_specs=tuple(out_specs),
        compiler_params=_params("arbitrary", "arbitrary"),
        name=name,
    )(*operands)
    return (res[0], res[1]) if emit_next else (res[0], None)


def _retention_kernel(lg_ref, q_ref, k_ref, v_ref, h_ref, wg_ref, gn_ref, o_ref,
                      state_ref, wgb_ref, gate_ref, *, heads_per_step):
    c_len = RET_CHUNK
    dk, dv = RET_QK_DIM, RET_V_DIM
    n_chunks = q_ref.shape[0] // c_len
    first_block_of_heads = jnp.logical_and(pl.program_id(1) == 0, pl.program_id(2) == 0)

    @pl.when(first_block_of_heads)
    def _():
        wgb_ref[...] = wg_ref[...].astype(BF16)

    @pl.when(pl.program_id(2) == 0)
    def _():
        state_ref[...] = jnp.zeros_like(state_ref)

    row = lax.broadcasted_iota(jnp.int32, (c_len, c_len), 0)
    col = lax.broadcasted_iota(jnp.int32, (c_len, c_len), 1)
    rel = (row - col).astype(F32)
    causal = rel >= 0
    rel_pos = jnp.where(causal, rel, 0.0)
    idx = lax.broadcasted_iota(jnp.int32, (c_len, 1), 0).astype(F32)
    decays = []
    for g in range(heads_per_step):
        lg = jnp.full((1, 1), lg_ref[pl.program_id(0) * heads_per_step + g], F32)
        decays.append((jnp.where(causal, jnp.exp(lg * rel_pos), 0.0),
                       jnp.exp(lg * (idx + 1.0)),
                       jnp.exp(lg * (c_len - 1.0 - idx)),
                       jnp.exp(lg * c_len)))

    for rows in _row_parts(h_ref.shape[0]):
        for g in range(heads_per_step):
            v_cols = slice(g * dv, (g + 1) * dv)
            gate_ref[rows, v_cols] = _silu(_bdot(h_ref[rows, :], wgb_ref[:, v_cols])).astype(gate_ref.dtype)

    for c in range(n_chunks):
        rows = slice(c * c_len, (c + 1) * c_len)
        for g in range(heads_per_step):
            decay_in, decay_q, decay_k, decay_chunk = decays[g]
            qk_cols = slice(g * dk, (g + 1) * dk)
            v_cols = slice(g * dv, (g + 1) * dv)
            q = q_ref[rows, qk_cols]
            k = k_ref[rows, qk_cols]
            v = v_ref[rows, v_cols]
            scores = lax.dot_general(q, k.astype(BF16), (((1,), (1,)), ((), ())),
                                     preferred_element_type=F32) * decay_in
            state = state_ref[g]
            y = _bdot(scores.astype(BF16), v) + _bdot(q, state.astype(BF16)) * decay_q
            kd_t = (k * decay_k).T.astype(BF16)
            state_ref[g] = state * decay_chunk + _bdot(kd_t, v)

            mu = jnp.mean(y, axis=-1, keepdims=True)
            yc = y - mu
            var = jnp.mean(yc * yc, axis=-1, keepdims=True)
            yn = yc * lax.rsqrt(var + NORM_EPS) * gn_ref[:, v_cols]
            o_ref[rows, v_cols] = (gate_ref[rows, v_cols] * yn).astype(o_ref.dtype)


def _retention_core(q, k, v, h, w_in, layer, gate_col0, gn_g, log_gamma, *, batch, seq):
    t, d = h.shape
    dk, dv = RET_QK_DIM, RET_V_DIM
    heads = v.shape[1] // dv
    hp = RET_HEADS_PER_STEP
    lb = RET_SEQ_BLOCK
    nb = seq // lb
    gate_off = gate_col0 // (hp * dv)
    kern = functools.partial(_retention_kernel, heads_per_step=hp)
    return pl.pallas_call(
        kern,
        out_shape=jax.ShapeDtypeStruct((t, heads * dv), BF16),
        grid=(heads // hp, batch, nb),
        in_specs=[pl.BlockSpec(memory_space=pltpu.SMEM),
                  pl.BlockSpec((lb, hp * dk), lambda h_, b, s: (b * nb + s, h_)),
                  pl.BlockSpec((lb, hp * dk), lambda h_, b, s: (b * nb + s, h_)),
                  pl.BlockSpec((lb, hp * dv), lambda h_, b, s: (b * nb + s, h_)),
                  pl.BlockSpec((lb, d), lambda h_, b, s: (b * nb + s, 0)),
                  pl.BlockSpec((None, d, hp * dv), lambda h_, b, s: (layer, 0, gate_off + h_)),
                  pl.BlockSpec((1, hp * dv), lambda h_, b, s: (0, h_))],
        out_specs=pl.BlockSpec((lb, hp * dv), lambda h_, b, s: (b * nb + s, h_)),
        scratch_shapes=[pltpu.VMEM((hp, dk, dv), F32),
                        pltpu.VMEM((d, hp * dv), BF16),
                        pltpu.VMEM((lb, hp * dv), BF16)],
        compiler_params=_params("arbitrary", "arbitrary", "arbitrary"),
        name="retention_core",
    )(log_gamma, q, k, v, h, w_in, gn_g)


def _mla_rope_table_kernel(pos_ref, inv_ref, cos_mask_ref, sin_sign_ref, tcos_ref, tsin_ref):
    ang = pos_ref[...].astype(F32) * inv_ref[...]
    tcos_ref[...] = jnp.cos(ang) * cos_mask_ref[...]
    tsin_ref[...] = jnp.sin(ang) * sin_sign_ref[...]


def _mla_rope_tables(pos_col):
    t = pos_col.shape[0]
    half = MLA_ROPE // 2
    inv = ROPE_THETA ** (-jnp.arange(0, MLA_ROPE, 2, dtype=F32) / MLA_ROPE)
    ones, zeros = jnp.ones((half,), F32), jnp.zeros((half,), F32)
    inv_row = jnp.concatenate([inv, inv, zeros, zeros]).reshape(1, LANES)
    cos_mask = jnp.concatenate([ones, ones, zeros, zeros]).reshape(1, LANES)
    sin_sign = jnp.concatenate([-ones, ones, zeros, zeros]).reshape(1, LANES)
    tm = ROW_TILE
    row = pl.BlockSpec((1, LANES), lambda i: (0, 0))
    return pl.pallas_call(
        _mla_rope_table_kernel,
        out_shape=(jax.ShapeDtypeStruct((t, LANES), F32), jax.ShapeDtypeStruct((t, LANES), F32)),
        grid=(t // tm,),
        in_specs=[pl.BlockSpec((tm, 1), lambda i: (i, 0)), row, row, row],
        out_specs=(pl.BlockSpec((tm, LANES), lambda i: (i, 0)),
                   pl.BlockSpec((tm, LANES), lambda i: (i, 0))),
        compiler_params=_params("parallel"),
        name="mla_rope_tables",
    )(pos_col, inv_row, cos_mask, sin_sign)


def _mla_in_kernel(h_ref, w_ref, gq_ref, gkv_ref, tcos_ref, tsin_ref,
                   cq_ref, ckv_ref, kr_ref, wb_ref, *, q_lora, kv_lora):
    @pl.when(pl.program_id(0) == 0)
    def _():
        wb_ref[...] = w_ref[...].astype(BF16)

    for rows in _row_parts(h_ref.shape[0]):
        c = _bdot(h_ref[rows, :], wb_ref[...])
        cq_ref[rows, :] = _rms(c[:, :q_lora], gq_ref[...]).astype(cq_ref.dtype)
        ckv_ref[rows, :] = _rms(c[:, q_lora:q_lora + kv_lora], gkv_ref[...]).astype(ckv_ref.dtype)
        kr = c[:, q_lora + kv_lora:]
        half = kr.shape[1] // 2
        v = jnp.concatenate([kr, jnp.zeros_like(kr)], axis=-1)
        lane = lax.broadcasted_iota(jnp.int32, v.shape, 1)
        swapped = jnp.where(lane < half, pltpu.roll(v, LANES - half, 1), pltpu.roll(v, half, 1))
        swapped = jnp.where(lane < 2 * half, swapped, 0.0)
        kr_ref[rows, :] = (v * tcos_ref[rows, :] + swapped * tsin_ref[rows, :]).astype(kr_ref.dtype)


def _mla_in(h, w, layer, gq, gkv, tcos, tsin):
    t, d = h.shape
    q_lora, kv_lora = gq.shape[1], gkv.shape[1]
    n_in = w.shape[2]
    tm = ROW_TILE
    kern = functools.partial(_mla_in_kernel, q_lora=q_lora, kv_lora=kv_lora)
    return pl.pallas_call(
        kern,
        out_shape=(jax.ShapeDtypeStruct((t, q_lora), BF16),
                   jax.ShapeDtypeStruct((t, kv_lora), BF16),
                   jax.ShapeDtypeStruct((t, LANES), BF16)),
        grid=(t // tm,),
        in_specs=[pl.BlockSpec((tm, d), lambda i: (i, 0)),
                  pl.BlockSpec((None, d, n_in), lambda i: (layer, 0, 0)),
                  pl.BlockSpec((1, q_lora), lambda i: (0, 0)),
                  pl.BlockSpec((1, kv_lora), lambda i: (0, 0)),
                  pl.BlockSpec((tm, LANES), lambda i: (i, 0)),
                  pl.BlockSpec((tm, LANES), lambda i: (i, 0))],
        out_specs=(pl.BlockSpec((tm, q_lora), lambda i: (i, 0)),
                   pl.BlockSpec((tm, kv_lora), lambda i: (i, 0)),
                   pl.BlockSpec((tm, LANES), lambda i: (i, 0))),
        scratch_shapes=[pltpu.VMEM((d, n_in), BF16)],
        compiler_params=_params("arbitrary"),
        name="mla_in",
    )(h, w, gq, gkv, tcos, tsin)


def _mla_q_weight_kernel(w_ref, o_ref):
    half = MLA_ROPE // 2
    w = w_ref[...]
    for t in range(o_ref.shape[0]):
        o = t * MLA_QD
        nope = w[:, o:o + MLA_NOPE]
        x1 = w[:, o + MLA_NOPE:o + MLA_NOPE + half]
        x2 = w[:, o + MLA_NOPE + half:o + MLA_QD]
        o_ref[t] = jnp.concatenate([nope, x1, x2, x2, x1], axis=-1).astype(o_ref.dtype)


def _mla_q_weights(w_uq, layer):
    q_lora = w_uq.shape[1]
    heads = w_uq.shape[2] // MLA_QD
    hp = 2
    return pl.pallas_call(
        _mla_q_weight_kernel,
        out_shape=jax.ShapeDtypeStruct((heads, q_lora, MLA_PAD), BF16),
        grid=(heads // hp,),
        in_specs=[pl.BlockSpec((None, q_lora, hp * MLA_QD), lambda h: (layer, 0, h))],
        out_specs=pl.BlockSpec((hp, q_lora, MLA_PAD), lambda h: (h, 0, 0)),
        compiler_params=_params("parallel"),
        name="mla_q_weights",
    )(w_uq)


def _mla_q_kernel(cq_ref, w_ref, tcos_ref, tsin_ref, o_ref, *, scale):
    cq = cq_ref[...]
    tcos, tsin = tcos_ref[...], tsin_ref[...]
    for g in range(o_ref.shape[0]):
        res = _bdot(cq, w_ref[g])
        rot = res[:, MLA_NOPE:]
        rope = rot * tcos + pltpu.roll(rot, LANES // 2, 1) * tsin
        o_ref[g, :, :MLA_NOPE] = (res[:, :MLA_NOPE] * scale).astype(o_ref.dtype)
        o_ref[g, :, MLA_NOPE:] = (rope * scale).astype(o_ref.dtype)


def _mla_q(cq, w_q, tcos, tsin):
    t, q_lora = cq.shape
    heads = w_q.shape[0]
    tm = ROW_TILE
    hp = min(MLA_Q_HEADS_PER_STEP, heads)
    kern = functools.partial(_mla_q_kernel, scale=MLA_QD ** -0.5 * LOG2_E)
    return pl.pallas_call(
        kern,
        out_shape=jax.ShapeDtypeStruct((heads, t, MLA_PAD), BF16),
        grid=(t // tm, heads // hp),
        in_specs=[pl.BlockSpec((tm, q_lora), lambda i, h: (i, 0)),
                  pl.BlockSpec((hp, q_lora, MLA_PAD), lambda i, h: (h, 0, 0)),
                  pl.BlockSpec((tm, LANES), lambda i, h: (i, 0)),
                  pl.BlockSpec((tm, LANES), lambda i, h: (i, 0))],
        out_specs=pl.BlockSpec((hp, tm, MLA_PAD), lambda i, h: (h, i, 0)),
        compiler_params=_params("parallel", "parallel"),
        name="mla_q",
    )(cq, w_q, tcos, tsin)


def _mla_kv_kernel(ckv_ref, w_ref, k_ref, v_ref):
    ckv = ckv_ref[...]
    per_head = MLA_NOPE + MLA_V
    for g in range(k_ref.shape[0]):
        kv = _bdot(ckv, w_ref[:, g * per_head:(g + 1) * per_head].astype(BF16))
        k_ref[g] = kv[:, :MLA_NOPE].astype(k_ref.dtype)
        v_ref[g] = kv[:, MLA_NOPE:].astype(v_ref.dtype)


def _mla_kv(ckv, w_ukv, layer):
    t, kv_lora = ckv.shape
    per_head = MLA_NOPE + MLA_V
    heads = w_ukv.shape[2] // per_head
    tm = ROW_TILE
    hp = min(MLA_HEADS_PER_STEP, heads)
    return pl.pallas_call(
        _mla_kv_kernel,
        out_shape=(jax.ShapeDtypeStruct((heads, t, MLA_NOPE), BF16),
                   jax.ShapeDtypeStruct((heads, t, MLA_V), BF16)),
        grid=(t // tm, heads // hp),
        in_specs=[pl.BlockSpec((tm, kv_lora), lambda i, h: (i, 0)),
                  pl.BlockSpec((None, kv_lora, hp * per_head), lambda i, h: (layer, 0, h))],
        out_specs=(pl.BlockSpec((hp, tm, MLA_NOPE), lambda i, h: (h, i, 0)),
                   pl.BlockSpec((hp, tm, MLA_V), lambda i, h: (h, i, 0))),
        compiler_params=_params("parallel", "parallel"),
        name="mla_kv",
    )(ckv, w_ukv)


def _flash_kernel(q_ref, kn_ref, kr_ref, vv_ref, side_ref, o_ref, side_o_ref,
                  k_ref, v_ref, sa_ref, sb_ref, m_ref, acc_ref):
    n_heads, seq, _ = q_ref.shape
    tile = ATTN_TILE
    nq = seq // tile
    tiles = [(qi, ki) for qi in range(nq) for ki in range(qi + 1)]
    s_bufs = (sa_ref, sb_ref)
    side_o_ref[...] = side_ref[...].astype(side_o_ref.dtype)

    def assemble(ki):
        for g in range(n_heads):
            k_ref[g, rows(ki), :MLA_NOPE] = kn_ref[g, rows(ki), :]
            k_ref[g, rows(ki), MLA_NOPE:] = kr_ref[rows(ki), :]
            v_ref[g, rows(ki), :MLA_V] = vv_ref[g, rows(ki), :]
            v_ref[g, rows(ki), MLA_V:] = jnp.ones((tile, v_ref.shape[2] - MLA_V), v_ref.dtype)

    def rows(i):
        return slice(i * tile, (i + 1) * tile)

    def row_parts(qi, ki):
        if ki != qi:
            return [(0, tile, tile)]
        return [(0, tile // 2, tile // 2), (tile // 2, tile, tile)]

    def scores_into(s_ref, qi, ki):
        for g in range(n_heads):
            for r0, r1, n_kv in row_parts(qi, ki):
                q = q_ref[g, qi * tile + r0:qi * tile + r1, :]
                k = k_ref[g, ki * tile:ki * tile + n_kv, :]
                s_ref[g, r0:r1, :n_kv] = lax.dot_general(q, k, (((1,), (1,)), ((), ())),
                                                         preferred_element_type=F32)

    def consume(s_ref, qi, ki):
        slot = qi % 2
        for g in range(n_heads):
            for r0, r1, n_kv in row_parts(qi, ki):
                s = s_ref[g, r0:r1, :n_kv]
                if ki == qi:
                    row = lax.broadcasted_iota(jnp.int32, s.shape, 0) + r0
                    col = lax.broadcasted_iota(jnp.int32, s.shape, 1)
                    s = jnp.where(col <= row, s, NEG_BIG)
                m_cur = jnp.max(s, axis=-1, keepdims=True)
                if ki == 0:
                    m_new = m_cur
                else:
                    m_prev = m_ref[slot, g, r0:r1]
                    m_new = jnp.maximum(m_prev, m_cur)
                    alpha = jnp.exp2(m_prev - m_new)
                p = jnp.exp2(s - m_new).astype(BF16)
                pv = _bdot(p, v_ref[g, ki * tile:ki * tile + n_kv, :])
                acc = pv if ki == 0 else alpha * acc_ref[slot, g, r0:r1] + pv
                if ki == qi:
                    out = (acc[:, :MLA_V] / acc[:, MLA_V:]).astype(o_ref.dtype)
                    o_ref[qi * tile + r0:qi * tile + r1, g * MLA_V:(g + 1) * MLA_V] = out
                else:
                    acc_ref[slot, g, r0:r1] = acc
                    m_ref[slot, g, r0:r1] = m_new

    def step(n):
        qi, ki = tiles[n]
        if n + 1 < len(tiles):
            next_qi, next_ki = tiles[n + 1]
            if next_ki == next_qi:
                assemble(next_ki)
            scores_into(s_bufs[(n + 1) % 2], next_qi, next_ki)
        consume(s_bufs[n % 2], qi, ki)

    def steps(first):
        for n in range(first, min(first + ATTN_STEPS_PER_REGION, len(tiles))):
            step(n)

    one_trip = jnp.minimum(pl.program_id(0) + 1, 1)
    assemble(0)
    scores_into(s_bufs[0], *tiles[0])
    for first in range(0, len(tiles), ATTN_STEPS_PER_REGION):
        lax.fori_loop(0, one_trip, lambda _, carry, first=first: (steps(first), carry)[1], 0)


def _flash_attention(q, k_nope, k_rope, v, side_w, layer, *, batch, seq):
    heads, t, _ = q.shape
    tile = ATTN_TILE
    hp = ATTN_HEADS_PER_STEP
    narrow_block = pl.BlockSpec((hp, seq, MLA_V), lambda b, h: (h, b, 0))
    grid = (batch, heads // hp)
    side_in, side_out, side_shape = _side_cast_specs(side_w, layer, grid[0] * grid[1], grid[1])
    return pl.pallas_call(
        _flash_kernel,
        out_shape=(jax.ShapeDtypeStruct((t, heads * MLA_V), BF16), side_shape),
        grid=grid,
        in_specs=[pl.BlockSpec((hp, seq, MLA_PAD), lambda b, h: (h, b, 0)),
                  narrow_block,
                  pl.BlockSpec((seq, LANES), lambda b, h: (b, 0)),
                  narrow_block,
                  side_in],
        out_specs=(pl.BlockSpec((seq, hp * MLA_V), lambda b, h: (b, h)), side_out),
        scratch_shapes=[pltpu.VMEM((hp, seq, MLA_PAD), BF16),
                        pltpu.VMEM((hp, seq, 2 * MLA_V), BF16),
                        pltpu.VMEM((hp, tile, tile), F32),
                        pltpu.VMEM((hp, tile, tile), F32),
                        pltpu.VMEM((2, hp, tile, 1), F32),
                        pltpu.VMEM((2, hp, tile, 2 * MLA_V), F32)],
        compiler_params=_params("arbitrary", "arbitrary"),
        name="mla_flash_attention",
    )(q, k_nope, k_rope, v, side_w)


def _retention_layer(x, h, g_post, g_next, w_in, gn_g, w_out, layer, cos, sin, log_gamma, *, batch, seq):
    d = x.shape[1]
    dk, dv = RET_QK_DIM, RET_V_DIM
    heads = d // dk
    n_q = heads * dk
    n_v = heads * dv
    tn = PROJ_COL_TILE
    q = _proj_rope(h, w_in, layer, cos, sin, col0=0, n_cols=n_q, tn=tn, head_dim=dk,
                   scale=None, out_dtype=BF16, name="ret_q_proj")
    k = _proj_rope(h, w_in, layer, cos, sin, col0=n_q, n_cols=n_q, tn=tn, head_dim=dk,
                   scale=dk ** -0.5, out_dtype=F32, name="ret_k_proj")
    v, w_out_b = _proj(h, w_in, layer, w_out, col0=2 * n_q, n_cols=n_v, tn=tn, out_dtype=BF16,
                       name="ret_v_proj")
    y = _retention_core(q, k, v, h, w_in, layer, 2 * n_q + n_v, gn_g, log_gamma, batch=batch, seq=seq)
    return _out_proj(y, w_out_b, x, g_post, g_next, n_k=1, name="ret_out_proj")


def _mla_layer(x, h, g_post, g_next, w_in, g_q, g_kv, w_uq, w_ukv, w_out, layer, tcos, tsin, *, batch, seq):
    cq, ckv, k_rope = _mla_in(h, w_in, layer, g_q, g_kv, tcos, tsin)
    q = _mla_q(cq, _mla_q_weights(w_uq, layer), tcos, tsin)
    k_nope, v = _mla_kv(ckv, w_ukv, layer)
    o, w_out_b = _flash_attention(q, k_nope, k_rope, v, w_out, layer, batch=batch, seq=seq)
    return _out_proj(o, w_out_b, x, g_post, g_next, n_k=1, name="mla_out_proj")


def _ffn(x, h, g_post, g_next, w_gu, w_down, layer):
    act, w_down_b = _proj_swiglu(h, w_gu, layer, w_down, tn=FFN_COL_TILE, name="ffn_gate_up")
    return _out_proj(act, w_down_b, x, g_post, g_next, n_k=FFN_DOWN_K_PARTS, name="ffn_down_proj")


def kernel(x, positions, norm_mix_pre, norm_mix_post, norm_ffn_pre, norm_ffn_post, ret_w_in, ret_gn_g, ret_w_out, mla_w_in, mla_g_q, mla_g_kv, mla_w_uq, mla_w_ukv, mla_w_out, ffn_w_gu, ffn_w_down):
    batch, seq, d = x.shape
    depth = norm_mix_pre.shape[0]
    t = batch * seq
    xf = x.reshape(t, d)
    pos_col = positions.reshape(t, 1)
    cos_r, sin_r = _rope_tables(pos_col, RET_QK_DIM)
    tcos_m, tsin_m = _mla_rope_tables(pos_col)
    ret_heads = d // RET_QK_DIM
    log_gamma = jnp.log1p(-jnp.exp2(-5.0 - jnp.arange(ret_heads, dtype=F32)))

    def row(a, i):
        return a[i].reshape(1, -1)

    h = _prenorm(xf, row(norm_mix_pre, 0))
    for i in range(depth):
        j = i // N_MIXERS
        g_ffn_pre = row(norm_ffn_pre, i)
        g_next_mix = row(norm_mix_pre, i + 1) if i + 1 < depth else None
        if i % N_MIXERS == 0:
            xf, h = _retention_layer(xf, h, row(norm_mix_post, i), g_ffn_pre, ret_w_in, row(ret_gn_g, j),
                                     ret_w_out, j, cos_r, sin_r, log_gamma, batch=batch, seq=seq)
        else:
            xf, h = _mla_layer(xf, h, row(norm_mix_post, i), g_ffn_pre, mla_w_in, row(mla_g_q, j),
                               row(mla_g_kv, j), mla_w_uq, mla_w_ukv, mla_w_out, j, tcos_m, tsin_m,
                               batch=batch, seq=seq)
        xf, h = _ffn(xf, h, row(norm_ffn_post, i), g_next_mix, ffn_w_gu, ffn_w_down, i)
    return xf.reshape(batch, seq, d)
```
